```python
import math
import jax, jax.numpy as jnp
from jax import lax
import numpy as np

D_MODEL = 1024
BATCH = 16
SEQ = 2048
DEPTH = 4

GRID_W = 64
D_FF = 2816
EPS = 1e-6
CONV_WIDTH = 256
CONV_K = 3
SSM_WIDTH = 256
SSM_GROUP = 16
SSM_GROUPS = SSM_WIDTH // SSM_GROUP
SSM_STATE = 64
NA_HEADS = 8
NA_HEAD_DIM = 64
NA_WIDTH = NA_HEADS * NA_HEAD_DIM
NA_ROWS_MAX = 8
NA_COLS = 16
NA_QBLOCK = 16
NA_KBAND = 32
N_COL_BLOCKS = GRID_W // NA_QBLOCK
RPB_ROWS = 2 * NA_ROWS_MAX - 1
RPB_COLS = 2 * NA_COLS - 1
N_BRANCH = 3
SPLIT_SIZES = (CONV_WIDTH, CONV_WIDTH, CONV_WIDTH, SSM_WIDTH, NA_WIDTH, NA_WIDTH, NA_WIDTH, D_MODEL, D_MODEL, D_MODEL)
IN_COLS = 3 * CONV_WIDTH + SSM_WIDTH + 3 * NA_WIDTH + N_BRANCH * D_MODEL

kernel_name = "hybrid_conv_s5_natten_macaron_encoder"


def rmsnorm(x, g):
    x32 = x.astype(jnp.float32)
    y = x32 * lax.rsqrt(jnp.mean(x32 * x32, axis=-1, keepdims=True) + EPS)
    return y.astype(x.dtype) * g


def swiglu(h, w_gate, w_up, w_down):
    return (jax.nn.silu(h @ w_gate) * (h @ w_up)) @ w_down


def short_conv(z, w):
    zp = jnp.pad(z, ((0, 0), (1, 1), (0, 0)))
    return w[0] * zp[:, :-2] + w[1] * zp[:, 1:-1] + w[2] * zp[:, 2:]


def _ssm_combine(e1, e2):
    a1r, a1i, b1r, b1i = e1
    a2r, a2i, b2r, b2i = e2
    return (a2r * a1r - a2i * a1i,
            a2r * a1i + a2i * a1r,
            a2r * b1r - a2i * b1i + b2r,
            a2r * b1i + a2i * b1r + b2i)


def s5_bidirectional(u, lam_re, lam_im, log_dt, b_re, b_im, c_re, c_im, d_skip):
    f32 = jnp.float32
    bsz, l, _ = u.shape
    u32 = u.astype(f32)
    ug = u32.reshape(bsz, l, SSM_GROUPS, SSM_GROUP)
    y = d_skip.astype(f32) * u32
    for direction in range(2):
        lr = jnp.minimum(lam_re[direction].astype(f32), -1e-4)
        li = lam_im[direction].astype(f32)
        dt = jnp.exp(log_dt[direction].astype(f32))[:, None]
        mag = jnp.exp(lr * dt)
        ab_re = mag * jnp.cos(li * dt)
        ab_im = mag * jnp.sin(li * dt)
        den = lr * lr + li * li
        nr, ni = ab_re - 1.0, ab_im
        f_re = (nr * lr + ni * li) / den
        f_im = (ni * lr - nr * li) / den
        br, bi = b_re[direction].astype(f32), b_im[direction].astype(f32)
        bb_re = f_re[..., None] * br - f_im[..., None] * bi
        bb_im = f_re[..., None] * bi + f_im[..., None] * br
        bu_re = jnp.einsum('blgh,gph->blgp', ug, bb_re)
        bu_im = jnp.einsum('blgh,gph->blgp', ug, bb_im)
        a_re = jnp.broadcast_to(ab_re[None, None], (1, l, SSM_GROUPS, SSM_STATE))
        a_im = jnp.broadcast_to(ab_im[None, None], (1, l, SSM_GROUPS, SSM_STATE))
        _, _, s_re, s_im = lax.associative_scan(
            _ssm_combine, (a_re, a_im, bu_re, bu_im), axis=1, reverse=(direction == 1))
        y_dir = (jnp.einsum('blgp,ghp->blgh', s_re, c_re[direction].astype(f32))
                 - jnp.einsum('blgp,ghp->blgh', s_im, c_im[direction].astype(f32)))
        y = y + y_dir.reshape(bsz, l, SSM_WIDTH)
    return y.astype(u.dtype)


def _na_column_tables():
    qcol = np.arange(GRID_W).reshape(N_COL_BLOCKS, NA_QBLOCK)
    band0 = np.clip(qcol[:, 0] - NA_COLS // 2, 0, GRID_W - NA_KBAND)
    kcol = band0[:, None] + np.arange(NA_KBAND)
    cs = np.clip(qcol - NA_COLS // 2, 0, GRID_W - NA_COLS)
    kc = kcol[:, None, :]
    valid = (kc >= cs[..., None]) & (kc < cs[..., None] + NA_COLS)
    dc_idx = np.clip(kc - qcol[..., None], -(NA_COLS - 1), NA_COLS - 1) + (NA_COLS - 1)
    return kcol, valid, dc_idx


def neighbourhood_attention(q, k, v, rpb):
    b, l = q.shape[0], q.shape[1]
    rows = l // GRID_W
    kr = min(NA_ROWS_MAX, rows)
    shp = (b, rows, GRID_W, NA_HEADS, NA_HEAD_DIM)
    q = q.reshape(shp) * (NA_HEAD_DIM ** -0.5)
    k = k.reshape(shp)
    v = v.reshape(shp)
    kcol, valid, dc_idx = _na_column_tables()
    valid = jnp.asarray(valid)

    def one_row(r):
        rs = jnp.clip(r - kr // 2, 0, rows - kr)
        k_blk = lax.dynamic_slice_in_dim(k, rs, kr, axis=1)[:, :, kcol]
        v_blk = lax.dynamic_slice_in_dim(v, rs, kr, axis=1)[:, :, kcol]
        q_r = lax.dynamic_index_in_dim(q, r, axis=1, keepdims=False)
        q_r = q_r.reshape(b, N_COL_BLOCKS, NA_QBLOCK, NA_HEADS, NA_HEAD_DIM)
        s = jnp.einsum('bnqhd,brnkhd->bhnqrk', q_r, k_blk).astype(jnp.float32)
        dr_idx = rs + jnp.arange(kr) - r + (NA_ROWS_MAX - 1)
        bias = rpb[:, dr_idx[None, None, :, None], dc_idx[:, :, None, :]]
        s = jnp.where(valid[:, :, None, :], s + bias.astype(jnp.float32), -1e30)
        shp_s = s.shape
        p = jax.nn.softmax(s.reshape(shp_s[:4] + (kr * NA_KBAND,)), axis=-1)
        p = p.reshape(shp_s).astype(v.dtype)
        o = jnp.einsum('bhnqrk,brnkhd->bnqhd', p, v_blk)
        return o.reshape(b, GRID_W, NA_WIDTH)

    out = lax.map(one_row, jnp.arange(rows))
    return jnp.moveaxis(out, 0, 1).reshape(b, l, NA_WIDTH)


def setup_inputs(seed: int = 0) -> dict:
    key = jax.random.key(seed)
    ks = iter(jax.random.split(key, 48))
    L, G, P, H = DEPTH, SSM_GROUPS, SSM_STATE, SSM_GROUP

    def nrm(shape, scale):
        return jax.random.normal(next(ks), shape, jnp.float32) * scale

    def gain():
        return 1.0 + nrm((L, D_MODEL), 0.02)

    n = jnp.arange(P, dtype=jnp.float32)
    inp = {}
    inp['x'] = nrm((BATCH, SEQ, D_MODEL), 1.0)
    inp['norm_ffn1_pre'] = gain()
    inp['norm_ffn1_post'] = gain()
    inp['ffn1_w_gate'] = nrm((L, D_MODEL, D_FF), D_MODEL ** -0.5)
    inp['ffn1_w_up'] = nrm((L, D_MODEL, D_FF), D_MODEL ** -0.5)
    inp['ffn1_w_down'] = nrm((L, D_FF, D_MODEL), D_FF ** -0.5)
    inp['norm_mix_pre'] = gain()
    inp['w_in'] = nrm((L, D_MODEL, IN_COLS), D_MODEL ** -0.5)
    inp['conv_w'] = nrm((L, CONV_K, CONV_WIDTH), CONV_K ** -0.5)
    inp['w_out_a'] = nrm((L, CONV_WIDTH, D_MODEL), CONV_WIDTH ** -0.5)
    inp['ssm_lam_re'] = -0.5 + nrm((L, 2, G, P), 0.01)
    inp['ssm_lam_im'] = jnp.pi * n + nrm((L, 2, G, P), 0.01)
    inp['ssm_log_dt'] = jax.random.uniform(next(ks), (L, 2, G), jnp.float32, math.log(1e-3), math.log(1e-1))
    inp['ssm_b_re'] = nrm((L, 2, G, P, H), (2 * H) ** -0.5)
    inp['ssm_b_im'] = nrm((L, 2, G, P, H), (2 * H) ** -0.5)
    inp['ssm_c_re'] = nrm((L, 2, G, H, P), (2 * P) ** -0.5)
    inp['ssm_c_im'] = nrm((L, 2, G, H, P), (2 * P) ** -0.5)
    inp['ssm_d'] = nrm((L, SSM_WIDTH), 1.0)
    inp['w_glu_a'] = nrm((L, SSM_WIDTH, D_MODEL), SSM_WIDTH ** -0.5)
    inp['w_glu_b'] = nrm((L, SSM_WIDTH, D_MODEL), SSM_WIDTH ** -0.5)
    inp['na_rpb'] = nrm((L, NA_HEADS, RPB_ROWS, RPB_COLS), 0.1)
    inp['w_out_c'] = nrm((L, NA_WIDTH, D_MODEL), NA_WIDTH ** -0.5)
    inp['w_o'] = nrm((L, D_MODEL, D_MODEL), D_MODEL ** -0.5)
    inp['norm_mix_post'] = gain()
    inp['norm_ffn2_pre'] = gain()
    inp['norm_ffn2_post'] = gain()
    inp['ffn2_w_gate'] = nrm((L, D_MODEL, D_FF), D_MODEL ** -0.5)
    inp['ffn2_w_up'] = nrm((L, D_MODEL, D_FF), D_MODEL ** -0.5)
    inp['ffn2_w_down'] = nrm((L, D_FF, D_MODEL), D_FF ** -0.5)
    return inp


def reference(x, norm_ffn1_pre, norm_ffn1_post, ffn1_w_gate, ffn1_w_up, ffn1_w_down,
              norm_mix_pre, w_in, conv_w, w_out_a,
              ssm_lam_re, ssm_lam_im, ssm_log_dt, ssm_b_re, ssm_b_im, ssm_c_re, ssm_c_im, ssm_d,
              w_glu_a, w_glu_b, na_rpb, w_out_c, w_o, norm_mix_post,
              norm_ffn2_pre, norm_ffn2_post, ffn2_w_gate, ffn2_w_up, ffn2_w_down):
    b, l, _ = x.shape
    offsets = [int(o) for o in np.cumsum(SPLIT_SIZES)[:-1]]
    for i in range(DEPTH):
        h = rmsnorm(x, norm_ffn1_pre[i])
        x = x + 0.5 * rmsnorm(swiglu(h, ffn1_w_gate[i], ffn1_w_up[i], ffn1_w_down[i]), norm_ffn1_post[i])

        h = rmsnorm(x, norm_mix_pre[i])
        proj = h @ w_in[i]
        a_b, a_c, a_v, s_u, q, k, v, g_a, g_b, g_c = jnp.split(proj, offsets, axis=-1)

        y_a = (a_b * short_conv(a_c * a_v, conv_w[i])) @ w_out_a[i]

        y_s = s5_bidirectional(s_u, ssm_lam_re[i], ssm_lam_im[i], ssm_log_dt[i],
                               ssm_b_re[i], ssm_b_im[i], ssm_c_re[i], ssm_c_im[i], ssm_d[i])
        z = jax.nn.gelu(y_s)
        y_b = (z @ w_glu_a[i]) * jax.nn.sigmoid(z @ w_glu_b[i])

        hs = (b, l, NA_HEADS, NA_HEAD_DIM)
        y_c = neighbourhood_attention(q.reshape(hs), k.reshape(hs), v.reshape(hs), na_rpb[i]) @ w_out_c[i]

        mix = jax.nn.sigmoid(g_a) * y_a + jax.nn.sigmoid(g_b) * y_b + jax.nn.sigmoid(g_c) * y_c
        x = x + rmsnorm(mix @ w_o[i], norm_mix_post[i])

        h = rmsnorm(x, norm_ffn2_pre[i])
        x = x + 0.5 * rmsnorm(swiglu(h, ffn2_w_gate[i], ffn2_w_up[i], ffn2_w_down[i]), norm_ffn2_post[i])
    return x
```

```python
import functools

import numpy as np
import jax
import jax.numpy as jnp
from jax import lax
from jax.experimental import pallas as pl
from jax.experimental.pallas import tpu as pltpu

F32 = jnp.float32
BF16 = jnp.bfloat16

D_MODEL = 1024
D_FF = 2816
EPS = 1e-6
CONV_WIDTH = 256
SSM_WIDTH = 256
SSM_GROUP = 16
SSM_GROUPS = SSM_WIDTH // SSM_GROUP
SSM_STATE = 64
SSM_LANES = SSM_GROUPS * SSM_STATE
NA_HEADS = 8
NA_HEAD_DIM = 64
NA_WIDTH = NA_HEADS * NA_HEAD_DIM
NA_ROWS = 8
NA_COLS = 16
GRID_W = 64
NA_KEYS = NA_ROWS * GRID_W
RPB_ROWS = 2 * NA_ROWS - 1
RPB_COLS = 2 * NA_COLS - 1
PROJ_COLS = 3 * CONV_WIDTH + SSM_WIDTH + 3 * NA_WIDTH
MASK_VALUE = -1e30

V7X_VMEM_BYTES = 64 * 1024 * 1024
V7X_LANES = 128
V7X_SUBLANES = 8

FFN_CHUNK = 256
ROW_TILE = 512
SSM_STEPS = 32


def _vmem_limit(nbytes):
    return int(min(V7X_VMEM_BYTES - 8 * 1024 * 1024, max(nbytes, 16 * 1024 * 1024)))


def _rms(x, g):
    ms = jnp.mean(x * x, axis=-1, keepdims=True)
    return x * lax.rsqrt(ms + EPS) * g


def _resident(shape):
    nd = len(shape)
    return pl.BlockSpec(shape, lambda *_: (0,) * nd, pipeline_mode=pl.Buffered(1))


def _ffn_kernel(x_ref, gpre_ref, gpost_ref, wg_ref, wu_ref, wd_ref, o_ref, acc_ref):
    x = x_ref[...]
    h = _rms(x, gpre_ref[...]).astype(BF16)
    for c in range(D_FF // FFN_CHUNK):
        sl = slice(c * FFN_CHUNK, (c + 1) * FFN_CHUNK)
        g = jnp.dot(h, wg_ref[:, sl], preferred_element_type=F32)
        u = jnp.dot(h, wu_ref[:, sl], preferred_element_type=F32)
        a = (g * jax.nn.sigmoid(g) * u).astype(BF16)
        d = jnp.dot(a, wd_ref[sl, :], preferred_element_type=F32)
        if c == 0:
            acc_ref[...] = d
        else:
            acc_ref[...] += d
    o_ref[...] = x + 0.5 * _rms(acc_ref[...], gpost_ref[...])


def _ffn(x, gpre, gpost, wg, wu, wd):
    t = x.shape[0]
    tm = ROW_TILE
    row = pl.BlockSpec((tm, D_MODEL), lambda i: (i, 0))
    est = 3 * D_MODEL * D_FF * 2 + 5 * tm * D_MODEL * 4 + 8 * tm * FFN_CHUNK * 4 + (4 << 20)
    return pl.pallas_call(
        _ffn_kernel,
        grid=(t // tm,),
        in_specs=[row, _resident((1, D_MODEL)), _resident((1, D_MODEL)),
                  _resident((D_MODEL, D_FF)), _resident((D_MODEL, D_FF)), _resident((D_FF, D_MODEL))],
        out_specs=row,
        out_shape=jax.ShapeDtypeStruct((t, D_MODEL), F32),
        scratch_shapes=[pltpu.VMEM((tm, D_MODEL), F32)],
        compiler_params=pltpu.CompilerParams(
            dimension_semantics=("parallel",), vmem_limit_bytes=_vmem_limit(est)),
        name="ffn",
    )(x, gpre, gpost, wg, wu, wd)


def _proj_kernel(x_ref, g_ref, w_ref, ab_ref, p_ref, su_ref, q_ref, k_ref, v_ref):
    h = _rms(x_ref[...], g_ref[...]).astype(BF16)
    cw = CONV_WIDTH
    ab_ref[...] = jnp.dot(h, w_ref[:, 0:cw], preferred_element_type=F32)
    a_c = jnp.dot(h, w_ref[:, cw:2 * cw], preferred_element_type=F32)
    a_v = jnp.dot(h, w_ref[:, 2 * cw:3 * cw], preferred_element_type=F32)
    p_ref[...] = a_c * a_v
    o = 3 * cw
    su_ref[...] = jnp.dot(h, w_ref[:, o:o + SSM_WIDTH], preferred_element_type=F32)
    o += SSM_WIDTH
    q = jnp.dot(h, w_ref[:, o:o + NA_WIDTH], preferred_element_type=F32)
    q_ref[...] = (q * (NA_HEAD_DIM ** -0.5)).astype(BF16)
    o += NA_WIDTH
    k_ref[...] = jnp.dot(h, w_ref[:, o:o + NA_WIDTH], preferred_element_type=F32).astype(BF16)
    o += NA_WIDTH
    v_ref[...] = jnp.dot(h, w_ref[:, o:o + NA_WIDTH], preferred_element_type=F32).astype(BF16)


def _proj(x, g, w):
    t = x.shape[0]
    tm = ROW_TILE

    def rows(c):
        return pl.BlockSpec((tm, c), lambda i: (i, 0))

    est = D_MODEL * PROJ_COLS * 2 + 2 * tm * (D_MODEL + 3 * CONV_WIDTH + 3 * NA_WIDTH) * 4 + (8 << 20)
    return pl.pallas_call(
        _proj_kernel,
        grid=(t // tm,),
        in_specs=[rows(D_MODEL), _resident((1, D_MODEL)), _resident((D_MODEL, PROJ_COLS))],
        out_specs=[rows(CONV_WIDTH), rows(CONV_WIDTH), rows(SSM_WIDTH),
                   rows(NA_WIDTH), rows(NA_WIDTH), rows(NA_WIDTH)],
        out_shape=[jax.ShapeDtypeStruct((t, CONV_WIDTH), F32), jax.ShapeDtypeStruct((t, CONV_WIDTH), F32),
                   jax.ShapeDtypeStruct((t, SSM_WIDTH), F32), jax.ShapeDtypeStruct((t, NA_WIDTH), BF16),
                   jax.ShapeDtypeStruct((t, NA_WIDTH), BF16), jax.ShapeDtypeStruct((t, NA_WIDTH), BF16)],
        compiler_params=pltpu.CompilerParams(
            dimension_semantics=("parallel",), vmem_limit_bytes=_vmem_limit(est)),
        name="proj",
    )(x, g, w)


def _ssm_kernel(uf_ref, ub_ref, lre_ref, lim_ref, ldt_ref, bre_ref, bim_ref, cre_ref, cim_ref,
                yf_ref, yb_ref, a_scr, bb_scr, bu_scr, st_scr, *, batch, steps):
    n = SSM_LANES
    halves = batch // V7X_SUBLANES

    @pl.when(pl.program_id(0) == 0)
    def _discretise():
        st_scr[...] = jnp.zeros_like(st_scr)
        for d in range(2):
            lr = jnp.minimum(lre_ref[d], -1e-4)
            li = lim_ref[d]
            dt = jnp.exp(ldt_ref[d])
            mag = jnp.exp(lr * dt)
            a_re = mag * jnp.cos(li * dt)
            a_im = mag * jnp.sin(li * dt)
            den = lr * lr + li * li
            nr, ni = a_re - 1.0, a_im
            f_re = (nr * lr + ni * li) / den
            f_im = (ni * lr - nr * li) / den
            a_scr[d, 0] = a_re
            a_scr[d, 1] = a_im
            fr, fi = f_re[0:1], f_im[0:1]
            br, bi = bre_ref[d], bim_ref[d]
            bb_scr[d, :, 0:n] = (fr * br - fi * bi).astype(BF16)
            bb_scr[d, :, n:2 * n] = (fr * bi + fi * br).astype(BF16)

    def run(d, u_ref, y_ref, reverse):
        bu_scr[...] = jnp.dot(u_ref[...].astype(BF16), bb_scr[d], preferred_element_type=F32)
        a_re = a_scr[d, 0]
        a_im = a_scr[d, 1]

        def body(k, carry):
            t = (steps - 1 - k) if reverse else k
            r0 = pl.multiple_of(t * batch, batch)
            new = []
            for hf in range(halves):
                s_re, s_im = carry[2 * hf], carry[2 * hf + 1]
                rows = pl.ds(r0 + V7X_SUBLANES * hf, V7X_SUBLANES)
                n_re = a_re * s_re - a_im * s_im + bu_scr[rows, 0:n]
                n_im = a_re * s_im + a_im * s_re + bu_scr[rows, n:2 * n]
                bu_scr[rows, 0:n] = n_re
                bu_scr[rows, n:2 * n] = n_im
                new += [n_re, n_im]
            return tuple(new)

        init = tuple(st_scr[d, j] for j in range(2 * halves))
        fin = lax.fori_loop(0, steps, body, init)
        for j in range(2 * halves):
            st_scr[d, j] = fin[j]
        s_re = bu_scr[:, 0:n].astype(BF16)
        s_im = bu_scr[:, n:2 * n].astype(BF16)
        y_ref[...] = (jnp.dot(s_re, cre_ref[d], preferred_element_type=F32)
                      - jnp.dot(s_im, cim_ref[d], preferred_element_type=F32))

    run(0, uf_ref, yf_ref, False)
    run(1, ub_ref, yb_ref, True)


def _ssm(su, lre, lim, ldt, bre, bim, cre, cim, batch):
    t = su.shape[0]
    steps = SSM_STEPS
    r = steps * batch
    nc = t // r
    n = SSM_LANES
    halves = batch // V7X_SUBLANES
    fwd = pl.BlockSpec((r, SSM_WIDTH), lambda i: (i, 0))
    bwd = pl.BlockSpec((r, SSM_WIDTH), lambda i: (nc - 1 - i, 0))
    est = (2 * (2 * SSM_WIDTH * n * 4 * 2 + 2 * n * SSM_WIDTH * 2 * 2) + 2 * SSM_WIDTH * 2 * n * 2
           + 3 * r * 2 * n * 4 + 8 * r * SSM_WIDTH * 4 + (8 << 20))
    return pl.pallas_call(
        functools.partial(_ssm_kernel, batch=batch, steps=steps),
        grid=(nc,),
        in_specs=[fwd, bwd] + [_resident((2, V7X_SUBLANES, n))] * 3
                 + [_resident((2, SSM_WIDTH, n))] * 2 + [_resident((2, n, SSM_WIDTH))] * 2,
        out_specs=[fwd, bwd],
        out_shape=[jax.ShapeDtypeStruct((t, SSM_WIDTH), F32)] * 2,
        scratch_shapes=[pltpu.VMEM((2, 2, V7X_SUBLANES, n), F32),
                        pltpu.VMEM((2, SSM_WIDTH, 2 * n), BF16),
                        pltpu.VMEM((r, 2 * n), F32),
                        pltpu.VMEM((2, 2 * halves, V7X_SUBLANES, n), F32)],
        compiler_params=pltpu.CompilerParams(
            dimension_semantics=("arbitrary",), vmem_limit_bytes=_vmem_limit(est)),
        name="ssm",
    )(su, su, lre, lim, ldt, bre, bim, cre, cim)


def _na_kernel(q_ref, k_ref, v_ref, bias_ref, o_ref, *, rows):
    lane = lax.broadcasted_iota(jnp.int32, (GRID_W, V7X_LANES), 1)
    first = lane < NA_HEAD_DIM
    pairs = NA_WIDTH // V7X_LANES
    zero = jnp.zeros((GRID_W, V7X_LANES), BF16)

    def row_body(r, carry):
        rs = jnp.clip(r - NA_ROWS // 2, 0, rows - NA_ROWS)
        off = r - rs
        q0 = pl.multiple_of(r * GRID_W, GRID_W)
        k0 = pl.multiple_of(rs * GRID_W, GRID_W)
        for j in range(pairs):
            cols = slice(j * V7X_LANES, (j + 1) * V7X_LANES)
            qp = q_ref[pl.ds(q0, GRID_W), cols]
            kw = k_ref[pl.ds(k0, NA_KEYS), cols]
            vw = v_ref[pl.ds(k0, NA_KEYS), cols]
            outs = []
            for hh in range(2):
                qm = jnp.where(first if hh == 0 else jnp.logical_not(first), qp, zero)
                s = lax.dot_general(qm, kw, (((1,), (1,)), ((), ())), preferred_element_type=F32)
                s = s + bias_ref[(2 * j + hh) * NA_ROWS + off]
                m = jnp.max(s, axis=-1, keepdims=True)
                p = jnp.exp(s - m)
                l = jnp.sum(p, axis=-1, keepdims=True)
                pv = jnp.dot(p.astype(BF16), vw, preferred_element_type=F32)
                outs.append(pv / l)
            o_ref[pl.ds(q0, GRID_W), cols] = jnp.where(first, outs[0], outs[1]).astype(BF16)
        return carry

    lax.fori_loop(0, rows, row_body, 0)


def _na(q, k, v, bias, batch):
    t = q.shape[0]
    seq = t // batch
    rows = seq // GRID_W
    shp = (seq, batch * NA_WIDTH)
    blk = pl.BlockSpec((seq, NA_WIDTH), lambda b: (0, b))
    est = 8 * seq * NA_WIDTH * 2 + bias.size * 4 + (8 << 20)
    out = pl.pallas_call(
        functools.partial(_na_kernel, rows=rows),
        grid=(batch,),
        in_specs=[blk, blk, blk, _resident(bias.shape)],
        out_specs=blk,
        out_shape=jax.ShapeDtypeStruct(shp, BF16),
        compiler_params=pltpu.CompilerParams(
            dimension_semantics=("parallel",), vmem_limit_bytes=_vmem_limit(est)),
        name="na",
    )(q.reshape(shp), k.reshape(shp), v.reshape(shp), bias)
    return out.reshape(t, NA_WIDTH)


def _na_bias_table(rpb):
    qc = np.arange(GRID_W)[:, None]
    kc = np.arange(GRID_W)[None, :]
    cs = np.clip(qc - NA_COLS // 2, 0, GRID_W - NA_COLS)
    valid = (kc >= cs) & (kc < cs + NA_COLS)
    dc = np.clip(kc - qc, -(NA_COLS - 1), NA_COLS - 1) + (NA_COLS - 1)
    off = np.arange(NA_ROWS)[:, None]
    dr = np.arange(NA_ROWS)[None, :] - off + (NA_ROWS - 1)
    tab = rpb[:, dr[:, None, :, None], dc[None, :, None, :]]
    tab = jnp.where(valid[None, None, :, None, :], tab, MASK_VALUE)
    return tab.reshape(NA_HEADS * NA_ROWS, GRID_W, NA_KEYS).astype(F32)


def _merge_kernel(x_ref, ab_ref, p_ref, pp_ref, pn_ref, su_ref, yf_ref, yb_ref, na_ref,
                  gpre_ref, gpost_ref, wgate_ref, convw_ref, woa_ref, dsk_ref, wga_ref, wgb_ref,
                  woc_ref, wo_ref, o_ref, *, batch):
    i = pl.program_id(0)
    x = x_ref[...]
    h = _rms(x, gpre_ref[...]).astype(BF16)

    p = p_ref[...]
    p_prev = jnp.where(i == 0, 0.0, pp_ref[...])
    p_next = jnp.where(i == pl.num_programs(0) - 1, 0.0, pn_ref[...])
    before = jnp.concatenate([p_prev, p[:-batch]], axis=0)
    after = jnp.concatenate([p[batch:], p_next], axis=0)
    cw = convw_ref[...]
    conv = cw[0:1] * before + cw[1:2] * p + cw[2:3] * after
    y_a = jnp.dot((ab_ref[...] * conv).astype(BF16), woa_ref[...], preferred_element_type=F32)
    g_a = jnp.dot(h, wgate_ref[:, 0:D_MODEL], preferred_element_type=F32)
    mix = jax.nn.sigmoid(g_a) * y_a

    y_s = dsk_ref[...] * su_ref[...] + yf_ref[...] + yb_ref[...]
    z = jax.nn.gelu(y_s).astype(BF16)
    y_b = (jnp.dot(z, wga_ref[...], preferred_element_type=F32)
           * jax.nn.sigmoid(jnp.dot(z, wgb_ref[...], preferred_element_type=F32)))
    g_b = jnp.dot(h, wgate_ref[:, D_MODEL:2 * D_MODEL], preferred_element_type=F32)
    mix = mix + jax.nn.sigmoid(g_b) * y_b

    y_c = jnp.dot(na_ref[...], woc_ref[...], preferred_element_type=F32)
    g_c = jnp.dot(h, wgate_ref[:, 2 * D_MODEL:3 * D_MODEL], preferred_element_type=F32)
    mix = mix + jax.nn.sigmoid(g_c) * y_c

    y = jnp.dot(mix.astype(BF16), wo_ref[...], preferred_element_type=F32)
    o_ref[...] = x + _rms(y, gpost_ref[...])


def _merge(x, ab, p, su, yf, yb, na, gpre, gpost, wgate, convw, woa, dsk, wga, wgb, woc, wo, batch):
    t = x.shape[0]
    tm = ROW_TILE
    per = tm // batch
    last = t // batch - 1

    def rows(c):
        return pl.BlockSpec((tm, c), lambda i: (i, 0))

    prev = pl.BlockSpec((batch, CONV_WIDTH), lambda i: (jnp.maximum(i * per - 1, 0), 0))
    nxt = pl.BlockSpec((batch, CONV_WIDTH), lambda i: (jnp.minimum((i + 1) * per, last), 0))
    wbytes = (3 * D_MODEL * D_MODEL + CONV_WIDTH * D_MODEL + 2 * SSM_WIDTH * D_MODEL
              + NA_WIDTH * D_MODEL + D_MODEL * D_MODEL) * 2
    est = wbytes + 2 * tm * (2 * D_MODEL + 5 * CONV_WIDTH) * 4 + 10 * tm * D_MODEL * 4 + (8 << 20)
    return pl.pallas_call(
        functools.partial(_merge_kernel, batch=batch),
        grid=(t // tm,),
        in_specs=[rows(D_MODEL), rows(CONV_WIDTH), rows(CONV_WIDTH), prev, nxt, rows(SSM_WIDTH),
                  rows(SSM_WIDTH), rows(SSM_WIDTH), rows(NA_WIDTH),
                  _resident((1, D_MODEL)), _resident((1, D_MODEL)), _resident((D_MODEL, 3 * D_MODEL)),
                  _resident((3, CONV_WIDTH)), _resident((CONV_WIDTH, D_MODEL)), _resident((1, SSM_WIDTH)),
                  _resident((SSM_WIDTH, D_MODEL)), _resident((SSM_WIDTH, D_MODEL)),
                  _resident((NA_WIDTH, D_MODEL)), _resident((D_MODEL, D_MODEL))],
        out_specs=rows(D_MODEL),
        out_shape=jax.ShapeDtypeStruct((t, D_MODEL), F32),
        compiler_params=pltpu.CompilerParams(
            dimension_semantics=("parallel",), vmem_limit_bytes=_vmem_limit(est)),
        name="merge",
    )(x, ab, p, p, p, su, yf, yb, na, gpre, gpost, wgate, convw, woa, dsk, wga, wgb, woc, wo)


def _state_rows(a):
    return jnp.broadcast_to(a.reshape(2, 1, SSM_LANES), (2, V7X_SUBLANES, SSM_LANES))


def _block_diag_in(b):
    eye = jnp.eye(SSM_GROUPS, dtype=b.dtype)
    out = jnp.swapaxes(b, 2, 3)[:, :, :, None, :] * eye[None, :, None, :, None]
    return out.reshape(2, SSM_WIDTH, SSM_LANES)


def _block_diag_out(c):
    eye = jnp.eye(SSM_GROUPS, dtype=c.dtype)
    out = jnp.swapaxes(c, 2, 3)[:, :, :, None, :] * eye[None, :, None, :, None]
    return out.reshape(2, SSM_LANES, SSM_WIDTH)


def kernel(x, norm_ffn1_pre, norm_ffn1_post, ffn1_w_gate, ffn1_w_up, ffn1_w_down, norm_mix_pre, w_in, conv_w, w_out_a, ssm_lam_re, ssm_lam_im, ssm_log_dt, ssm_b_re, ssm_b_im, ssm_c_re, ssm_c_im, ssm_d, w_glu_a, w_glu_b, na_rpb, w_out_c, w_o, norm_mix_post, norm_ffn2_pre, norm_ffn2_post, ffn2_w_gate, ffn2_w_up, ffn2_w_down):
    batch, seq, d = x.shape
    depth = w_in.shape[0]
    t = batch * seq
    assert d == D_MODEL and batch % V7X_SUBLANES == 0 and seq % GRID_W == 0
    assert seq // GRID_W >= NA_ROWS and t % ROW_TILE == 0 and ROW_TILE % batch == 0
    assert seq % SSM_STEPS == 0

    def row(v):
        return v.reshape(1, -1)

    xt = jnp.transpose(x, (1, 0, 2)).reshape(t, d)
    for i in range(depth):
        xt = _ffn(xt, row(norm_ffn1_pre[i]), row(norm_ffn1_post[i]), ffn1_w_gate[i].astype(BF16),
                  ffn1_w_up[i].astype(BF16), ffn1_w_down[i].astype(BF16))

        w_in_b = w_in[i].astype(BF16)
        ab, p, su, q, k, v = _proj(xt, row(norm_mix_pre[i]), w_in_b[:, :PROJ_COLS])
        ldt = jnp.broadcast_to(ssm_log_dt[i][:, :, None], (2, SSM_GROUPS, SSM_STATE))
        yf, yb = _ssm(su, _state_rows(ssm_lam_re[i]), _state_rows(ssm_lam_im[i]), _state_rows(ldt),
                      _block_diag_in(ssm_b_re[i]), _block_diag_in(ssm_b_im[i]),
                      _block_diag_out(ssm_c_re[i]).astype(BF16), _block_diag_out(ssm_c_im[i]).astype(BF16),
                      batch)
        na = _na(q, k, v, _na_bias_table(na_rpb[i]), batch)
        xt = _merge(xt, ab, p, su, yf, yb, na, row(norm_mix_pre[i]), row(norm_mix_post[i]),
                    w_in_b[:, PROJ_COLS:], conv_w[i], w_out_a[i].astype(BF16), row(ssm_d[i]),
                    w_glu_a[i].astype(BF16), w_glu_b[i].astype(BF16), w_out_c[i].astype(BF16),
                    w_o[i].astype(BF16), batch)

        xt = _ffn(xt, row(norm_ffn2_pre[i]), row(norm_ffn2_post[i]), ffn2_w_gate[i].astype(BF16),
                  ffn2_w_up[i].astype(BF16), ffn2_w_down[i].astype(BF16))
    return jnp.transpose(xt.reshape(seq, batch, d), (1, 0, 2))
```

```python
import functools

import numpy as np
import jax
import jax.numpy as jnp
from jax import lax
from jax.experimental import pallas as pl
from jax.experimental.pallas import tpu as pltpu

F32 = jnp.float32
BF16 = jnp.bfloat16

D_MODEL = 1024
D_FF = 2816
EPS = 1e-6
CONV_WIDTH = 256
SSM_WIDTH = 256
SSM_GROUP = 16
SSM_GROUPS = SSM_WIDTH // SSM_GROUP
SSM_STATE = 64
SSM_LANES = SSM_GROUPS * SSM_STATE
NA_HEADS = 8
NA_HEAD_DIM = 64
NA_WIDTH = NA_HEADS * NA_HEAD_DIM
NA_ROWS = 8
NA_COLS = 16
GRID_W = 64
NA_KEYS = NA_ROWS * GRID_W
RPB_COLS = 2 * NA_COLS - 1
PROJ_COLS = 3 * CONV_WIDTH + SSM_WIDTH + 3 * NA_WIDTH
MASK_VALUE = -1e30

V7X_VMEM_BYTES = 64 * 1024 * 1024
V7X_LANES = 128
V7X_SUBLANES = 8

NA_PAIRS = NA_WIDTH // V7X_LANES
FFN_CHUNK = 256
ROW_TILE = 512
SSM_STEPS = 32


def _vmem_limit(nbytes):
    return int(min(V7X_VMEM_BYTES - 8 * 1024 * 1024, max(nbytes, 16 * 1024 * 1024)))


def _rms(x, g):
    ms = jnp.mean(x * x, axis=-1, keepdims=True)
    return x * lax.rsqrt(ms + EPS) * g


def _resident(shape):
    nd = len(shape)
    return pl.BlockSpec(shape, lambda *_: (0,) * nd, pipeline_mode=pl.Buffered(1))


def _ffn_kernel(x_ref, gpre_ref, gpost_ref, wg_ref, wu_ref, wd_ref, o_ref, acc_ref):
    x = x_ref[...]
    h = _rms(x, gpre_ref[...]).astype(BF16)
    for c in range(D_FF // FFN_CHUNK):
        sl = slice(c * FFN_CHUNK, (c + 1) * FFN_CHUNK)
        g = jnp.dot(h, wg_ref[:, sl], preferred_element_type=F32)
        u = jnp.dot(h, wu_ref[:, sl], preferred_element_type=F32)
        a = (g * jax.nn.sigmoid(g) * u).astype(BF16)
        d = jnp.dot(a, wd_ref[sl, :], preferred_element_type=F32)
        if c == 0:
            acc_ref[...] = d
        else:
            acc_ref[...] += d
    o_ref[...] = x + 0.5 * _rms(acc_ref[...], gpost_ref[...])


def _ffn(x, gpre, gpost, wg, wu, wd):
    t = x.shape[0]
    tm = ROW_TILE
    row = pl.BlockSpec((tm, D_MODEL), lambda i: (i, 0))
    est = 3 * D_MODEL * D_FF * 2 + 5 * tm * D_MODEL * 4 + 8 * tm * FFN_CHUNK * 4 + (4 << 20)
    return pl.pallas_call(
        _ffn_kernel,
        grid=(t // tm,),
        in_specs=[row, _resident((1, D_MODEL)), _resident((1, D_MODEL)),
                  _resident((D_MODEL, D_FF)), _resident((D_MODEL, D_FF)), _resident((D_FF, D_MODEL))],
        out_specs=row,
        out_shape=jax.ShapeDtypeStruct((t, D_MODEL), F32),
        scratch_shapes=[pltpu.VMEM((tm, D_MODEL), F32)],
        compiler_params=pltpu.CompilerParams(
            dimension_semantics=("parallel",), vmem_limit_bytes=_vmem_limit(est)),
        name="ffn",
    )(x, gpre, gpost, wg, wu, wd)


def _proj_kernel(x_ref, g_ref, w_ref, ab_ref, p_ref, su_ref, q_ref, k_ref, v_ref):
    h = _rms(x_ref[...], g_ref[...]).astype(BF16)
    cw = CONV_WIDTH
    ab_ref[...] = jnp.dot(h, w_ref[:, 0:cw], preferred_element_type=F32)
    a_c = jnp.dot(h, w_ref[:, cw:2 * cw], preferred_element_type=F32)
    a_v = jnp.dot(h, w_ref[:, 2 * cw:3 * cw], preferred_element_type=F32)
    p_ref[...] = a_c * a_v
    o = 3 * cw
    su_ref[...] = jnp.dot(h, w_ref[:, o:o + SSM_WIDTH], preferred_element_type=F32)
    o += SSM_WIDTH
    q = jnp.dot(h, w_ref[:, o:o + NA_WIDTH], preferred_element_type=F32)
    q_ref[...] = (q * (NA_HEAD_DIM ** -0.5)).astype(BF16)
    o += NA_WIDTH
    k_ref[...] = jnp.dot(h, w_ref[:, o:o + NA_WIDTH], preferred_element_type=F32).astype(BF16)
    o += NA_WIDTH
    v_ref[...] = jnp.dot(h, w_ref[:, o:o + NA_WIDTH], preferred_element_type=F32).astype(BF16)


def _proj(x, g, w):
    t = x.shape[0]
    tm = ROW_TILE

    def rows(c):
        return pl.BlockSpec((tm, c), lambda i: (i, 0))

    est = D_MODEL * PROJ_COLS * 2 + 2 * tm * (D_MODEL + 3 * CONV_WIDTH + 3 * NA_WIDTH) * 4 + (8 << 20)
    return pl.pallas_call(
        _proj_kernel,
        grid=(t // tm,),
        in_specs=[rows(D_MODEL), _resident((1, D_MODEL)), _resident((D_MODEL, PROJ_COLS))],
        out_specs=[rows(CONV_WIDTH), rows(CONV_WIDTH), rows(SSM_WIDTH),
                   rows(NA_WIDTH), rows(NA_WIDTH), rows(NA_WIDTH)],
        out_shape=[jax.ShapeDtypeStruct((t, CONV_WIDTH), F32), jax.ShapeDtypeStruct((t, CONV_WIDTH), F32),
                   jax.ShapeDtypeStruct((t, SSM_WIDTH), F32), jax.ShapeDtypeStruct((t, NA_WIDTH), BF16),
                   jax.ShapeDtypeStruct((t, NA_WIDTH), BF16), jax.ShapeDtypeStruct((t, NA_WIDTH), BF16)],
        compiler_params=pltpu.CompilerParams(
            dimension_semantics=("parallel",), vmem_limit_bytes=_vmem_limit(est)),
        name="proj",
    )(x, g, w)


def _ssm_kernel(uf_ref, ub_ref, perm_ref, lre_ref, lim_ref, ldt_ref, bre_ref, bim_ref, cre_ref, cim_ref,
                yf_ref, yb_ref, a_scr, bb_scr, bu_scr, st_scr, *, batch, steps):
    n = SSM_LANES
    halves = batch // V7X_SUBLANES
    rows_n = batch * steps

    @pl.when(pl.program_id(0) == 0)
    def _discretise():
        st_scr[...] = jnp.zeros_like(st_scr)
        for d in range(2):
            lr = jnp.minimum(lre_ref[d], -1e-4)
            li = lim_ref[d]
            dt = jnp.exp(ldt_ref[d])
            mag = jnp.exp(lr * dt)
            a_re = mag * jnp.cos(li * dt)
            a_im = mag * jnp.sin(li * dt)
            den = lr * lr + li * li
            nr, ni = a_re - 1.0, a_im
            f_re = (nr * lr + ni * li) / den
            f_im = (ni * lr - nr * li) / den
            a_scr[d, 0] = a_re
            a_scr[d, 1] = a_im
            fr, fi = f_re[0:1], f_im[0:1]
            br, bi = bre_ref[d], bim_ref[d]
            bb_scr[d, :, 0:n] = (fr * br - fi * bi).astype(BF16)
            bb_scr[d, :, n:2 * n] = (fr * bi + fi * br).astype(BF16)

    def run(d, u_ref, y_ref, reverse):
        u = u_ref[...].reshape(rows_n, SSM_WIDTH).astype(BF16)
        u = jnp.dot(perm_ref[0], u, preferred_element_type=F32).astype(BF16)
        bu_scr[...] = jnp.dot(u, bb_scr[d], preferred_element_type=F32)
        a_re = a_scr[d, 0]
        a_im = a_scr[d, 1]

        def body(k, carry):
            t = (steps - 1 - k) if reverse else k
            r0 = pl.multiple_of(t * batch, batch)
            new = []
            for hf in range(halves):
                s_re, s_im = carry[2 * hf], carry[2 * hf + 1]
                rows = pl.ds(r0 + V7X_SUBLANES * hf, V7X_SUBLANES)
                n_re = a_re * s_re - a_im * s_im + bu_scr[rows, 0:n]
                n_im = a_re * s_im + a_im * s_re + bu_scr[rows, n:2 * n]
                bu_scr[rows, 0:n] = n_re
                bu_scr[rows, n:2 * n] = n_im
                new += [n_re, n_im]
            return tuple(new)

        init = tuple(st_scr[d, j] for j in range(2 * halves))
        fin = lax.fori_loop(0, steps, body, init)
        for j in range(2 * halves):
            st_scr[d, j] = fin[j]
        s_re = bu_scr[:, 0:n].astype(BF16)
        s_im = bu_scr[:, n:2 * n].astype(BF16)
        y = (jnp.dot(s_re, cre_ref[d], preferred_element_type=F32)
             - jnp.dot(s_im, cim_ref[d], preferred_element_type=F32))
        hi = y.astype(BF16)
        lo = (y - hi.astype(F32)).astype(BF16)
        y = (jnp.dot(perm_ref[1], hi, preferred_element_type=F32)
             + jnp.dot(perm_ref[1], lo, preferred_element_type=F32))
        y_ref[...] = y.reshape(batch, steps, SSM_WIDTH)

    run(0, uf_ref, yf_ref, False)
    run(1, ub_ref, yb_ref, True)


def _row_permutations(batch, steps):
    r = np.arange(batch * steps)
    src = (r % batch) * steps + r // batch
    fwd = np.zeros((batch * steps, batch * steps), np.float32)
    fwd[r, src] = 1.0
    return jnp.asarray(np.stack([fwd, fwd.T]), BF16)


def _ssm(su, lre, lim, ldt, bre, bim, cre, cim, batch):
    t = su.shape[0]
    seq = t // batch
    steps = SSM_STEPS
    r = steps * batch
    nc = seq // steps
    n = SSM_LANES
    halves = batch // V7X_SUBLANES
    fwd = pl.BlockSpec((batch, steps, SSM_WIDTH), lambda i: (0, i, 0))
    bwd = pl.BlockSpec((batch, steps, SSM_WIDTH), lambda i: (0, nc - 1 - i, 0))
    est = (2 * SSM_WIDTH * n * 4 * 2 + 2 * n * SSM_WIDTH * 2 * 2 + 2 * SSM_WIDTH * 2 * n * 2
           + 3 * r * 2 * n * 4 + 8 * r * SSM_WIDTH * 4 + 2 * r * r * 2 + (8 << 20))
    su3 = su.reshape(batch, seq, SSM_WIDTH)
    yf, yb = pl.pallas_call(
        functools.partial(_ssm_kernel, batch=batch, steps=steps),
        grid=(nc,),
        in_specs=[fwd, bwd, _resident((2, r, r))] + [_resident((2, V7X_SUBLANES, n))] * 3
                 + [_resident((2, SSM_WIDTH, n))] * 2 + [_resident((2, n, SSM_WIDTH))] * 2,
        out_specs=[fwd, bwd],
        out_shape=[jax.ShapeDtypeStruct((batch, seq, SSM_WIDTH), F32)] * 2,
        scratch_shapes=[pltpu.VMEM((2, 2, V7X_SUBLANES, n), F32),
                        pltpu.VMEM((2, SSM_WIDTH, 2 * n), BF16),
                        pltpu.VMEM((r, 2 * n), F32),
                        pltpu.VMEM((2, 2 * halves, V7X_SUBLANES, n), F32)],
        compiler_params=pltpu.CompilerParams(
            dimension_semantics=("arbitrary",), vmem_limit_bytes=_vmem_limit(est)),
        name="ssm",
    )(su3, su3, _row_permutations(batch, steps), lre, lim, ldt, bre, bim, cre, cim)
    return yf.reshape(t, SSM_WIDTH), yb.reshape(t, SSM_WIDTH)


def _na_kernel(q_ref, k_ref, v_ref, bias_ref, o_ref, s_scr, *, rows):
    lane = lax.broadcasted_iota(jnp.int32, (GRID_W, V7X_LANES), 1)
    first = lane < NA_HEAD_DIM
    zero = jnp.zeros((GRID_W, V7X_LANES), BF16)

    def window(r):
        rs = jnp.clip(r - NA_ROWS // 2, 0, rows - NA_ROWS)
        return r - rs, pl.multiple_of(r * GRID_W, GRID_W), pl.multiple_of(rs * GRID_W, GRID_W)

    def scores(r, buf):
        off, q0, k0 = window(r)
        for j in range(NA_PAIRS):
            cols = slice(j * V7X_LANES, (j + 1) * V7X_LANES)
            qp = q_ref[pl.ds(q0, GRID_W), cols]
            q2 = jnp.concatenate([jnp.where(first, qp, zero), jnp.where(first, zero, qp)], axis=0)
            kw = k_ref[pl.ds(k0, NA_KEYS), cols]
            s = lax.dot_general(q2, kw, (((1,), (1,)), ((), ())), preferred_element_type=F32)
            s_scr[buf, j] = s + bias_ref[j * NA_ROWS + off]

    def attend(r, buf):
        _, q0, k0 = window(r)
        for j in range(NA_PAIRS):
            cols = slice(j * V7X_LANES, (j + 1) * V7X_LANES)
            s = s_scr[buf, j]
            p = jnp.exp(s - jnp.max(s, axis=-1, keepdims=True))
            inv = 1.0 / jnp.sum(p, axis=-1, keepdims=True)
            vw = v_ref[pl.ds(k0, NA_KEYS), cols]
            pv = jnp.dot(p.astype(BF16), vw, preferred_element_type=F32) * inv
            o_ref[pl.ds(q0, GRID_W), cols] = jnp.where(first, pv[:GRID_W], pv[GRID_W:]).astype(BF16)

    scores(0, 0)

    def body(rr, carry):
        r = 2 * rr
        scores(r + 1, 1)
        attend(r, 0)
        scores(jnp.minimum(r + 2, rows - 1), 0)
        attend(r + 1, 1)
        return carry

    lax.fori_loop(0, rows // 2, body, 0)


def _na(q, k, v, bias, batch):
    t = q.shape[0]
    seq = t // batch
    rows = seq // GRID_W
    blk = pl.BlockSpec((seq, NA_WIDTH), lambda b: (b, 0))
    est = 8 * seq * NA_WIDTH * 2 + bias.size * 4 + 2 * NA_PAIRS * 2 * GRID_W * NA_KEYS * 4 + (12 << 20)
    return pl.pallas_call(
        functools.partial(_na_kernel, rows=rows),
        grid=(batch,),
        in_specs=[blk, blk, blk, _resident(bias.shape)],
        out_specs=blk,
        out_shape=jax.ShapeDtypeStruct((t, NA_WIDTH), BF16),
        scratch_shapes=[pltpu.VMEM((2, NA_PAIRS, 2 * GRID_W, NA_KEYS), F32)],
        compiler_params=pltpu.CompilerParams(
            dimension_semantics=("parallel",), vmem_limit_bytes=_vmem_limit(est)),
        name="na",
    )(q, k, v, bias)


def _na_bias_table(rpb):
    by_off = jnp.stack([rpb[:, NA_ROWS - 1 - off:2 * NA_ROWS - 1 - off, :] for off in range(NA_ROWS)],
                       axis=1)
    pad = GRID_W - NA_COLS
    ext = jnp.concatenate([jnp.repeat(by_off[..., :1], pad, axis=-1), by_off,
                           jnp.repeat(by_off[..., -1:], pad, axis=-1)], axis=-1)
    tab = jnp.stack([ext[..., GRID_W - 1 - qc:2 * GRID_W - 1 - qc] for qc in range(GRID_W)],
                    axis=2)
    qc = np.arange(GRID_W)[:, None]
    kc = np.arange(GRID_W)[None, :]
    cs = np.clip(qc - NA_COLS // 2, 0, GRID_W - NA_COLS)
    valid = (kc >= cs) & (kc < cs + NA_COLS)
    tab = jnp.where(valid[None, None, :, None, :], tab, MASK_VALUE)
    tab = tab.reshape(NA_PAIRS, 2, NA_ROWS, GRID_W, NA_KEYS)
    tab = jnp.transpose(tab, (0, 2, 1, 3, 4))
    return tab.reshape(NA_PAIRS * NA_ROWS, 2 * GRID_W, NA_KEYS).astype(F32)


def _merge_kernel(x_ref, ab_ref, p_ref, pp_ref, pn_ref, su_ref, yf_ref, yb_ref, na_ref,
                  gpre_ref, gpost_ref, wgate_ref, convw_ref, woa_ref, dsk_ref, wga_ref, wgb_ref,
                  woc_ref, wo_ref, o_ref, *, tiles_per_seq):
    i = pl.program_id(0)
    x = x_ref[...]
    tm = x.shape[0]
    h = _rms(x, gpre_ref[...]).astype(BF16)

    p = p_ref[...]
    row = lax.broadcasted_iota(jnp.int32, p.shape, 0)
    at_start = i % tiles_per_seq == 0
    at_end = i % tiles_per_seq == tiles_per_seq - 1
    p_prev = jnp.where(at_start, 0.0, pp_ref[V7X_SUBLANES - 1:V7X_SUBLANES, :])
    p_next = jnp.where(at_end, 0.0, pn_ref[0:1, :])
    before = jnp.where(row == 0, p_prev, pltpu.roll(p, 1, axis=0))
    after = jnp.where(row == tm - 1, p_next, pltpu.roll(p, tm - 1, axis=0))
    cw = convw_ref[...]
    conv = cw[0:1] * before + cw[1:2] * p + cw[2:3] * after
    y_a = jnp.dot((ab_ref[...] * conv).astype(BF16), woa_ref[...], preferred_element_type=F32)
    g_a = jnp.dot(h, wgate_ref[:, 0:D_MODEL], preferred_element_type=F32)
    mix = jax.nn.sigmoid(g_a) * y_a

    y_s = dsk_ref[...] * su_ref[...] + yf_ref[...] + yb_ref[...]
    z = jax.nn.gelu(y_s).astype(BF16)
    y_b = (jnp.dot(z, wga_ref[...], preferred_element_type=F32)
           * jax.nn.sigmoid(jnp.dot(z, wgb_ref[...], preferred_element_type=F32)))
    g_b = jnp.dot(h, wgate_ref[:, D_MODEL:2 * D_MODEL], preferred_element_type=F32)
    mix = mix + jax.nn.sigmoid(g_b) * y_b

    y_c = jnp.dot(na_ref[...], woc_ref[...], preferred_element_type=F32)
    g_c = jnp.dot(h, wgate_ref[:, 2 * D_MODEL:3 * D_MODEL], preferred_element_type=F32)
    mix = mix + jax.nn.sigmoid(g_c) * y_c

    y = jnp.dot(mix.astype(BF16), wo_ref[...], preferred_element_type=F32)
    o_ref[...] = x + _rms(y, gpost_ref[...])


def _merge(x, ab, p, su, yf, yb, na, gpre, gpost, wgate, convw, woa, dsk, wga, wgb, woc, wo, seq):
    t = x.shape[0]
    tm = ROW_TILE
    per = tm // V7X_SUBLANES
    last = t // V7X_SUBLANES - 1

    def rows(c):
        return pl.BlockSpec((tm, c), lambda i: (i, 0))

    prev = pl.BlockSpec((V7X_SUBLANES, CONV_WIDTH), lambda i: (jnp.maximum(i * per - 1, 0), 0))
    nxt = pl.BlockSpec((V7X_SUBLANES, CONV_WIDTH), lambda i: (jnp.minimum((i + 1) * per, last), 0))
    wbytes = (3 * D_MODEL * D_MODEL + CONV_WIDTH * D_MODEL + 2 * SSM_WIDTH * D_MODEL
              + NA_WIDTH * D_MODEL + D_MODEL * D_MODEL) * 2
    est = wbytes + 2 * tm * (2 * D_MODEL + 5 * CONV_WIDTH) * 4 + 10 * tm * D_MODEL * 4 + (8 << 20)
    return pl.pallas_call(
        functools.partial(_merge_kernel, tiles_per_seq=seq // tm),
        grid=(t // tm,),
        in_specs=[rows(D_MODEL), rows(CONV_WIDTH), rows(CONV_WIDTH), prev, nxt, rows(SSM_WIDTH),
                  rows(SSM_WIDTH), rows(SSM_WIDTH), rows(NA_WIDTH),
                  _resident((1, D_MODEL)), _resident((1, D_MODEL)), _resident((D_MODEL, 3 * D_MODEL)),
                  _resident((3, CONV_WIDTH)), _resident((CONV_WIDTH, D_MODEL)), _resident((1, SSM_WIDTH)),
                  _resident((SSM_WIDTH, D_MODEL)), _resident((SSM_WIDTH, D_MODEL)),
                  _resident((NA_WIDTH, D_MODEL)), _resident((D_MODEL, D_MODEL))],
        out_specs=rows(D_MODEL),
        out_shape=jax.ShapeDtypeStruct((t, D_MODEL), F32),
        compiler_params=pltpu.CompilerParams(
            dimension_semantics=("parallel",), vmem_limit_bytes=_vmem_limit(est)),
        name="merge",
    )(x, ab, p, p, p, su, yf, yb, na, gpre, gpost, wgate, convw, woa, dsk, wga, wgb, woc, wo)


def _state_rows(a):
    return jnp.broadcast_to(a.reshape(2, 1, SSM_LANES), (2, V7X_SUBLANES, SSM_LANES))


def _block_diag_in(b):
    eye = jnp.eye(SSM_GROUPS, dtype=b.dtype)
    out = jnp.swapaxes(b, 2, 3)[:, :, :, None, :] * eye[None, :, None, :, None]
    return out.reshape(2, SSM_WIDTH, SSM_LANES)


def _block_diag_out(c):
    eye = jnp.eye(SSM_GROUPS, dtype=c.dtype)
    out = jnp.swapaxes(c, 2, 3)[:, :, :, None, :] * eye[None, :, None, :, None]
    return out.reshape(2, SSM_LANES, SSM_WIDTH)


def kernel(x, norm_ffn1_pre, norm_ffn1_post, ffn1_w_gate, ffn1_w_up, ffn1_w_down, norm_mix_pre, w_in, conv_w, w_out_a, ssm_lam_re, ssm_lam_im, ssm_log_dt, ssm_b_re, ssm_b_im, ssm_c_re, ssm_c_im, ssm_d, w_glu_a, w_glu_b, na_rpb, w_out_c, w_o, norm_mix_post, norm_ffn2_pre, norm_ffn2_post, ffn2_w_gate, ffn2_w_up, ffn2_w_down):
    batch, seq, d = x.shape
    depth = w_in.shape[0]
    t = batch * seq
    assert d == D_MODEL and batch % V7X_SUBLANES == 0 and seq % GRID_W == 0
    rows = seq // GRID_W
    assert rows >= NA_ROWS and rows % 2 == 0 and seq % ROW_TILE == 0 and seq % SSM_STEPS == 0

    def row(v):
        return v.reshape(1, -1)

    xt = x.reshape(t, d)
    for i in range(depth):
        xt = _ffn(xt, row(norm_ffn1_pre[i]), row(norm_ffn1_post[i]), ffn1_w_gate[i].astype(BF16),
                  ffn1_w_up[i].astype(BF16), ffn1_w_down[i].astype(BF16))

        w_in_b = w_in[i].astype(BF16)
        ab, p, su, q, k, v = _proj(xt, row(norm_mix_pre[i]), w_in_b[:, :PROJ_COLS])
        ldt = jnp.broadcast_to(ssm_log_dt[i][:, :, None], (2, SSM_GROUPS, SSM_STATE))
        yf, yb = _ssm(su, _state_rows(ssm_lam_re[i]), _state_rows(ssm_lam_im[i]), _state_rows(ldt),
                      _block_diag_in(ssm_b_re[i]), _block_diag_in(ssm_b_im[i]),
                      _block_diag_out(ssm_c_re[i]).astype(BF16), _block_diag_out(ssm_c_im[i]).astype(BF16),
                      batch)
        na = _na(q, k, v, _na_bias_table(na_rpb[i]), batch)
        xt = _merge(xt, ab, p, su, yf, yb, na, row(norm_mix_pre[i]), row(norm_mix_post[i]),
                    w_in_b[:, PROJ_COLS:], conv_w[i], w_out_a[i].astype(BF16), row(ssm_d[i]),
                    w_glu_a[i].astype(BF16), w_glu_b[i].astype(BF16), w_out_c[i].astype(BF16),
                    w_o[i].astype(BF16), seq)

        xt = _ffn(xt, row(norm_ffn2_pre[i]), row(norm_ffn2_post[i]), ffn2_w_gate[i].astype(BF16),
                  ffn2_w_up[i].astype(BF16), ffn2_w_down[i].astype(BF16))
    return xt.reshape(batch, seq, d)
```

```python
import functools

import numpy as np
import jax
import jax.numpy as jnp
from jax import lax
from jax.experimental import pallas as pl
from jax.experimental.pallas import tpu as pltpu

F32 = jnp.float32
BF16 = jnp.bfloat16

D_MODEL = 1024
D_FF = 2816
EPS = 1e-6
CONV_WIDTH = 256
SSM_WIDTH = 256
SSM_GROUP = 16
SSM_GROUPS = SSM_WIDTH // SSM_GROUP
SSM_STATE = 64
SSM_LANES = SSM_GROUPS * SSM_STATE
NA_HEADS = 8
NA_HEAD_DIM = 64
NA_WIDTH = NA_HEADS * NA_HEAD_DIM
NA_ROWS = 8
NA_COLS = 16
GRID_W = 64
NA_KEYS = NA_ROWS * GRID_W
RPB_COLS = 2 * NA_COLS - 1
PROJ_COLS = 3 * CONV_WIDTH + SSM_WIDTH + 3 * NA_WIDTH
MASK_VALUE = -1e30

V7X_VMEM_BYTES = 64 * 1024 * 1024
V7X_LANES = 128
V7X_SUBLANES = 8

NA_PAIRS = NA_WIDTH // V7X_LANES
FFN_CHUNK = 256
ROW_TILE = 512
SSM_STEPS = 32


def _vmem_limit(nbytes):
    return int(min(V7X_VMEM_BYTES - 8 * 1024 * 1024, max(nbytes, 16 * 1024 * 1024)))


def _rms(x, g):
    ms = jnp.mean(x * x, axis=-1, keepdims=True)
    return x * lax.rsqrt(ms + EPS) * g


def _resident(shape):
    nd = len(shape)
    return pl.BlockSpec(shape, lambda *_: (0,) * nd, pipeline_mode=pl.Buffered(1))


def _ffn_kernel(x_ref, gpre_ref, gpost_ref, wg_ref, wu_ref, wd_ref, o_ref, acc_ref):
    x = x_ref[...]
    h = _rms(x, gpre_ref[...]).astype(BF16)
    for c in range(D_FF // FFN_CHUNK):
        sl = slice(c * FFN_CHUNK, (c + 1) * FFN_CHUNK)
        g = jnp.dot(h, wg_ref[:, sl], preferred_element_type=F32)
        u = jnp.dot(h, wu_ref[:, sl], preferred_element_type=F32)
        a = (g * jax.nn.sigmoid(g) * u).astype(BF16)
        d = jnp.dot(a, wd_ref[sl, :], preferred_element_type=F32)
        if c == 0:
            acc_ref[...] = d
        else:
            acc_ref[...] += d
    o_ref[...] = x + 0.5 * _rms(acc_ref[...], gpost_ref[...])


def _ffn(x, gpre, gpost, wg, wu, wd):
    t = x.shape[0]
    tm = ROW_TILE
    row = pl.BlockSpec((tm, D_MODEL), lambda i: (i, 0))
    est = 3 * D_MODEL * D_FF * 2 + 5 * tm * D_MODEL * 4 + 8 * tm * FFN_CHUNK * 4 + (4 << 20)
    return pl.pallas_call(
        _ffn_kernel,
        grid=(t // tm,),
        in_specs=[row, _resident((1, D_MODEL)), _resident((1, D_MODEL)),
                  _resident((D_MODEL, D_FF)), _resident((D_MODEL, D_FF)), _resident((D_FF, D_MODEL))],
        out_specs=row,
        out_shape=jax.ShapeDtypeStruct((t, D_MODEL), F32),
        scratch_shapes=[pltpu.VMEM((tm, D_MODEL), F32)],
        compiler_params=pltpu.CompilerParams(
            dimension_semantics=("parallel",), vmem_limit_bytes=_vmem_limit(est)),
        name="ffn",
    )(x, gpre, gpost, wg, wu, wd)


def _proj_kernel(x_ref, g_ref, w_ref, ab_ref, p_ref, su_ref, q_ref, k_ref, v_ref):
    h = _rms(x_ref[...], g_ref[...]).astype(BF16)
    cw = CONV_WIDTH
    ab_ref[...] = jnp.dot(h, w_ref[:, 0:cw], preferred_element_type=F32)
    a_c = jnp.dot(h, w_ref[:, cw:2 * cw], preferred_element_type=F32)
    a_v = jnp.dot(h, w_ref[:, 2 * cw:3 * cw], preferred_element_type=F32)
    p_ref[...] = a_c * a_v
    o = 3 * cw
    su_ref[...] = jnp.dot(h, w_ref[:, o:o + SSM_WIDTH], preferred_element_type=F32)
    o += SSM_WIDTH
    q = jnp.dot(h, w_ref[:, o:o + NA_WIDTH], preferred_element_type=F32)
    q_ref[...] = (q * (NA_HEAD_DIM ** -0.5)).astype(BF16)
    o += NA_WIDTH
    k_ref[...] = jnp.dot(h, w_ref[:, o:o + NA_WIDTH], preferred_element_type=F32).astype(BF16)
    o += NA_WIDTH
    v_ref[...] = jnp.dot(h, w_ref[:, o:o + NA_WIDTH], preferred_element_type=F32).astype(BF16)


def _proj(x, g, w):
    t = x.shape[0]
    tm = ROW_TILE

    def rows(c):
        return pl.BlockSpec((tm, c), lambda i: (i, 0))

    est = D_MODEL * PROJ_COLS * 2 + 2 * tm * (D_MODEL + 3 * CONV_WIDTH + 3 * NA_WIDTH) * 4 + (8 << 20)
    return pl.pallas_call(
        _proj_kernel,
        grid=(t // tm,),
        in_specs=[rows(D_MODEL), _resident((1, D_MODEL)), _resident((D_MODEL, PROJ_COLS))],
        out_specs=[rows(CONV_WIDTH), rows(CONV_WIDTH), rows(SSM_WIDTH),
                   rows(NA_WIDTH), rows(NA_WIDTH), rows(NA_WIDTH)],
        out_shape=[jax.ShapeDtypeStruct((t, CONV_WIDTH), F32), jax.ShapeDtypeStruct((t, CONV_WIDTH), F32),
                   jax.ShapeDtypeStruct((t, SSM_WIDTH), F32), jax.ShapeDtypeStruct((t, NA_WIDTH), BF16),
                   jax.ShapeDtypeStruct((t, NA_WIDTH), BF16), jax.ShapeDtypeStruct((t, NA_WIDTH), BF16)],
        compiler_params=pltpu.CompilerParams(
            dimension_semantics=("parallel",), vmem_limit_bytes=_vmem_limit(est)),
        name="proj",
    )(x, g, w)


def _ssm_kernel(uf_ref, ub_ref, perm_ref, lre_ref, lim_ref, ldt_ref, bre_ref, bim_ref, cre_ref, cim_ref,
                yf_ref, yb_ref, a_scr, bb_scr, bu_scr, st_scr, *, batch, steps):
    n = SSM_LANES
    halves = batch // V7X_SUBLANES
    rows_n = batch * steps

    @pl.when(pl.program_id(0) == 0)
    def _discretise():
        st_scr[...] = jnp.zeros_like(st_scr)
        for d in range(2):
            lr = jnp.minimum(lre_ref[d], -1e-4)
            li = lim_ref[d]
            dt = jnp.exp(ldt_ref[d])
            mag = jnp.exp(lr * dt)
            a_re = mag * jnp.cos(li * dt)
            a_im = mag * jnp.sin(li * dt)
            den = lr * lr + li * li
            nr, ni = a_re - 1.0, a_im
            f_re = (nr * lr + ni * li) / den
            f_im = (ni * lr - nr * li) / den
            a_scr[d, 0] = a_re
            a_scr[d, 1] = a_im
            fr, fi = f_re[0:1], f_im[0:1]
            br, bi = bre_ref[d], bim_ref[d]
            bb_scr[d, :, 0:n] = (fr * br - fi * bi).astype(BF16)
            bb_scr[d, :, n:2 * n] = (fr * bi + fi * br).astype(BF16)

    def run(d, u_ref, y_ref, reverse):
        u = u_ref[...].reshape(rows_n, SSM_WIDTH).astype(BF16)
        u = jnp.dot(perm_ref[0], u, preferred_element_type=F32).astype(BF16)
        bu = bu_scr.at[d]
        bu[...] = jnp.dot(u, bb_scr[d], preferred_element_type=F32)
        a_re = a_scr[d, 0]
        a_im = a_scr[d, 1]

        def body(k, carry):
            t = (steps - 1 - k) if reverse else k
            r0 = pl.multiple_of(t * batch, batch)
            new = []
            for hf in range(halves):
                s_re, s_im = carry[2 * hf], carry[2 * hf + 1]
                rows = pl.ds(r0 + V7X_SUBLANES * hf, V7X_SUBLANES)
                n_re = a_re * s_re - a_im * s_im + bu[rows, 0:n]
                n_im = a_re * s_im + a_im * s_re + bu[rows, n:2 * n]
                bu[rows, 0:n] = n_re
                bu[rows, n:2 * n] = n_im
                new += [n_re, n_im]
            return tuple(new)

        init = tuple(st_scr[d, j] for j in range(2 * halves))
        fin = lax.fori_loop(0, steps, body, init, unroll=True)
        for j in range(2 * halves):
            st_scr[d, j] = fin[j]
        s_re = bu[:, 0:n].astype(BF16)
        s_im = bu[:, n:2 * n].astype(BF16)
        y = (jnp.dot(s_re, cre_ref[d], preferred_element_type=F32)
             - jnp.dot(s_im, cim_ref[d], preferred_element_type=F32))
        hi = y.astype(BF16)
        lo = (y - hi.astype(F32)).astype(BF16)
        y = (jnp.dot(perm_ref[1], hi, preferred_element_type=F32)
             + jnp.dot(perm_ref[1], lo, preferred_element_type=F32))
        y_ref[...] = y.reshape(batch, steps, SSM_WIDTH)

    run(0, uf_ref, yf_ref, False)
    run(1, ub_ref, yb_ref, True)


def _row_permutations(batch, steps):
    r = np.arange(batch * steps)
    src = (r % batch) * steps + r // batch
    fwd = np.zeros((batch * steps, batch * steps), np.float32)
    fwd[r, src] = 1.0
    return jnp.asarray(np.stack([fwd, fwd.T]), BF16)


def _ssm(su, lre, lim, ldt, bre, bim, cre, cim, batch):
    t = su.shape[0]
    seq = t // batch
    steps = SSM_STEPS
    r = steps * batch
    nc = seq // steps
    n = SSM_LANES
    halves = batch // V7X_SUBLANES
    fwd = pl.BlockSpec((batch, steps, SSM_WIDTH), lambda i: (0, i, 0))
    bwd = pl.BlockSpec((batch, steps, SSM_WIDTH), lambda i: (0, nc - 1 - i, 0))
    est = (2 * SSM_WIDTH * n * 4 * 2 + 2 * n * SSM_WIDTH * 2 * 2 + 2 * SSM_WIDTH * 2 * n * 2
           + 3 * r * 2 * n * 4 + 8 * r * SSM_WIDTH * 4 + 2 * r * r * 2 + (8 << 20))
    su3 = su.reshape(batch, seq, SSM_WIDTH)
    yf, yb = pl.pallas_call(
        functools.partial(_ssm_kernel, batch=batch, steps=steps),
        grid=(nc,),
        in_specs=[fwd, bwd, _resident((2, r, r))] + [_resident((2, V7X_SUBLANES, n))] * 3
                 + [_resident((2, SSM_WIDTH, n))] * 2 + [_resident((2, n, SSM_WIDTH))] * 2,
        out_specs=[fwd, bwd],
        out_shape=[jax.ShapeDtypeStruct((batch, seq, SSM_WIDTH), F32)] * 2,
        scratch_shapes=[pltpu.VMEM((2, 2, V7X_SUBLANES, n), F32),
                        pltpu.VMEM((2, SSM_WIDTH, 2 * n), BF16),
                        pltpu.VMEM((2, r, 2 * n), F32),
                        pltpu.VMEM((2, 2 * halves, V7X_SUBLANES, n), F32)],
        compiler_params=pltpu.CompilerParams(
            dimension_semantics=("arbitrary",), vmem_limit_bytes=_vmem_limit(est)),
        name="ssm",
    )(su3, su3, _row_permutations(batch, steps), lre, lim, ldt, bre, bim, cre, cim)
    return yf.reshape(t, SSM_WIDTH), yb.reshape(t, SSM_WIDTH)


def _na_kernel(q_ref, k_ref, v_ref, bias_ref, o_ref, s_scr, *, rows):
    lane = lax.broadcasted_iota(jnp.int32, (GRID_W, V7X_LANES), 1)
    first = lane < NA_HEAD_DIM
    zero = jnp.zeros((GRID_W, V7X_LANES), BF16)

    def window(r):
        rs = jnp.clip(r - NA_ROWS // 2, 0, rows - NA_ROWS)
        return r - rs, pl.multiple_of(r * GRID_W, GRID_W), pl.multiple_of(rs * GRID_W, GRID_W)

    def scores(r, buf):
        off, q0, k0 = window(r)
        for j in range(NA_PAIRS):
            cols = slice(j * V7X_LANES, (j + 1) * V7X_LANES)
            qp = q_ref[pl.ds(q0, GRID_W), cols]
            q2 = jnp.concatenate([jnp.where(first, qp, zero), jnp.where(first, zero, qp)], axis=0)
            kw = k_ref[pl.ds(k0, NA_KEYS), cols]
            s = lax.dot_general(q2, kw, (((1,), (1,)), ((), ())), preferred_element_type=F32)
            s_scr[buf, j] = s + bias_ref[j * NA_ROWS + off]

    def attend(r, buf):
        _, q0, k0 = window(r)
        for j in range(NA_PAIRS):
            cols = slice(j * V7X_LANES, (j + 1) * V7X_LANES)
            s = s_scr[buf, j]
            p = jnp.exp(s - jnp.max(s, axis=-1, keepdims=True))
            inv = 1.0 / jnp.sum(p, axis=-1, keepdims=True)
            vw = v_ref[pl.ds(k0, NA_KEYS), cols]
            pv = jnp.dot(p.astype(BF16), vw, preferred_element_type=F32) * inv
            o_ref[pl.ds(q0, GRID_W), cols] = jnp.where(first, pv[:GRID_W], pv[GRID_W:]).astype(BF16)

    scores(0, 0)

    def body(rr, carry):
        r = 2 * rr
        scores(r + 1, 1)
        attend(r, 0)
        scores(jnp.minimum(r + 2, rows - 1), 0)
        attend(r + 1, 1)
        return carry

    lax.fori_loop(0, rows // 2, body, 0)


def _na(q, k, v, bias, batch):
    t = q.shape[0]
    seq = t // batch
    rows = seq // GRID_W
    blk = pl.BlockSpec((seq, NA_WIDTH), lambda b: (b, 0))
    est = 8 * seq * NA_WIDTH * 2 + bias.size * 4 + 2 * NA_PAIRS * 2 * GRID_W * NA_KEYS * 4 + (12 << 20)
    return pl.pallas_call(
        functools.partial(_na_kernel, rows=rows),
        grid=(batch,),
        in_specs=[blk, blk, blk, _resident(bias.shape)],
        out_specs=blk,
        out_shape=jax.ShapeDtypeStruct((t, NA_WIDTH), BF16),
        scratch_shapes=[pltpu.VMEM((2, NA_PAIRS, 2 * GRID_W, NA_KEYS), F32)],
        compiler_params=pltpu.CompilerParams(
            dimension_semantics=("parallel",), vmem_limit_bytes=_vmem_limit(est)),
        name="na",
    )(q, k, v, bias)


def _na_bias_table(rpb):
    by_off = jnp.stack([rpb[:, NA_ROWS - 1 - off:2 * NA_ROWS - 1 - off, :] for off in range(NA_ROWS)],
                       axis=1)
    pad = GRID_W - NA_COLS
    ext = jnp.concatenate([jnp.repeat(by_off[..., :1], pad, axis=-1), by_off,
                           jnp.repeat(by_off[..., -1:], pad + 1, axis=-1)], axis=-1)
    lead = ext.shape[:-1]
    skew = jnp.broadcast_to(ext[..., None, :], lead + (GRID_W, 2 * GRID_W)).reshape(lead + (-1,))
    skew = skew[..., :GRID_W * (2 * GRID_W - 1)].reshape(lead + (GRID_W, 2 * GRID_W - 1))
    tab = jnp.swapaxes(skew[..., GRID_W - 1:], 2, 3)
    qc = np.arange(GRID_W)[:, None]
    kc = np.arange(GRID_W)[None, :]
    cs = np.clip(qc - NA_COLS // 2, 0, GRID_W - NA_COLS)
    valid = (kc >= cs) & (kc < cs + NA_COLS)
    tab = jnp.where(valid[None, None, :, None, :], tab, MASK_VALUE)
    tab = tab.reshape(NA_PAIRS, 2, NA_ROWS, GRID_W, NA_KEYS)
    tab = jnp.transpose(tab, (0, 2, 1, 3, 4))
    return tab.reshape(NA_PAIRS * NA_ROWS, 2 * GRID_W, NA_KEYS).astype(F32)


def _merge_kernel(x_ref, ab_ref, p_ref, pp_ref, pn_ref, su_ref, yf_ref, yb_ref, na_ref,
                  gpre_ref, gpost_ref, wgate_ref, convw_ref, woa_ref, dsk_ref, wga_ref, wgb_ref,
                  woc_ref, wo_ref, o_ref, *, tiles_per_seq):
    i = pl.program_id(0)
    x = x_ref[...]
    tm = x.shape[0]
    h = _rms(x, gpre_ref[...]).astype(BF16)

    p = p_ref[...]
    row = lax.broadcasted_iota(jnp.int32, p.shape, 0)
    at_start = i % tiles_per_seq == 0
    at_end = i % tiles_per_seq == tiles_per_seq - 1
    p_prev = jnp.where(at_start, 0.0, pp_ref[V7X_SUBLANES - 1:V7X_SUBLANES, :])
    p_next = jnp.where(at_end, 0.0, pn_ref[0:1, :])
    before = jnp.where(row == 0, p_prev, pltpu.roll(p, 1, axis=0))
    after = jnp.where(row == tm - 1, p_next, pltpu.roll(p, tm - 1, axis=0))
    cw = convw_ref[...]
    conv = cw[0:1] * before + cw[1:2] * p + cw[2:3] * after
    y_a = jnp.dot((ab_ref[...] * conv).astype(BF16), woa_ref[...], preferred_element_type=F32)
    g_a = jnp.dot(h, wgate_ref[:, 0:D_MODEL], preferred_element_type=F32)
    mix = jax.nn.sigmoid(g_a) * y_a

    y_s = dsk_ref[...] * su_ref[...] + yf_ref[...] + yb_ref[...]
    z = jax.nn.gelu(y_s).astype(BF16)
    y_b = (jnp.dot(z, wga_ref[...], preferred_element_type=F32)
           * jax.nn.sigmoid(jnp.dot(z, wgb_ref[...], preferred_element_type=F32)))
    g_b = jnp.dot(h, wgate_ref[:, D_MODEL:2 * D_MODEL], preferred_element_type=F32)
    mix = mix + jax.nn.sigmoid(g_b) * y_b

    y_c = jnp.dot(na_ref[...], woc_ref[...], preferred_element_type=F32)
    g_c = jnp.dot(h, wgate_ref[:, 2 * D_MODEL:3 * D_MODEL], preferred_element_type=F32)
    mix = mix + jax.nn.sigmoid(g_c) * y_c

    y = jnp.dot(mix.astype(BF16), wo_ref[...], preferred_element_type=F32)
    o_ref[...] = x + _rms(y, gpost_ref[...])


def _merge(x, ab, p, su, yf, yb, na, gpre, gpost, wgate, convw, woa, dsk, wga, wgb, woc, wo, seq):
    t = x.shape[0]
    tm = ROW_TILE
    per = tm // V7X_SUBLANES
    last = t // V7X_SUBLANES - 1

    def rows(c):
        return pl.BlockSpec((tm, c), lambda i: (i, 0))

    prev = pl.BlockSpec((V7X_SUBLANES, CONV_WIDTH), lambda i: (jnp.maximum(i * per - 1, 0), 0))
    nxt = pl.BlockSpec((V7X_SUBLANES, CONV_WIDTH), lambda i: (jnp.minimum((i + 1) * per, last), 0))
    wbytes = (3 * D_MODEL * D_MODEL + CONV_WIDTH * D_MODEL + 2 * SSM_WIDTH * D_MODEL
              + NA_WIDTH * D_MODEL + D_MODEL * D_MODEL) * 2
    est = wbytes + 2 * tm * (2 * D_MODEL + 5 * CONV_WIDTH) * 4 + 10 * tm * D_MODEL * 4 + (8 << 20)
    return pl.pallas_call(
        functools.partial(_merge_kernel, tiles_per_seq=seq // tm),
        grid=(t // tm,),
        in_specs=[rows(D_MODEL), rows(CONV_WIDTH), rows(CONV_WIDTH), prev, nxt, rows(SSM_WIDTH),
                  rows(SSM_WIDTH), rows(SSM_WIDTH), rows(NA_WIDTH),
                  _resident((1, D_MODEL)), _resident((1, D_MODEL)), _resident((D_MODEL, 3 * D_MODEL)),
                  _resident((3, CONV_WIDTH)), _resident((CONV_WIDTH, D_MODEL)), _resident((1, SSM_WIDTH)),
                  _resident((SSM_WIDTH, D_MODEL)), _resident((SSM_WIDTH, D_MODEL)),
                  _resident((NA_WIDTH, D_MODEL)), _resident((D_MODEL, D_MODEL))],
        out_specs=rows(D_MODEL),
        out_shape=jax.ShapeDtypeStruct((t, D_MODEL), F32),
        compiler_params=pltpu.CompilerParams(
            dimension_semantics=("parallel",), vmem_limit_bytes=_vmem_limit(est)),
        name="merge",
    )(x, ab, p, p, p, su, yf, yb, na, gpre, gpost, wgate, convw, woa, dsk, wga, wgb, woc, wo)


def _state_rows(a):
    return jnp.broadcast_to(a.reshape(2, 1, SSM_LANES), (2, V7X_SUBLANES, SSM_LANES))


def _block_diag_in(b):
    eye = jnp.eye(SSM_GROUPS, dtype=b.dtype)
    out = jnp.swapaxes(b, 2, 3)[:, :, :, None, :] * eye[None, :, None, :, None]
    return out.reshape(2, SSM_WIDTH, SSM_LANES)


def _block_diag_out(c):
    eye = jnp.eye(SSM_GROUPS, dtype=c.dtype)
    out = jnp.swapaxes(c, 2, 3)[:, :, :, None, :] * eye[None, :, None, :, None]
    return out.reshape(2, SSM_LANES, SSM_WIDTH)


def kernel(x, norm_ffn1_pre, norm_ffn1_post, ffn1_w_gate, ffn1_w_up, ffn1_w_down, norm_mix_pre, w_in, conv_w, w_out_a, ssm_lam_re, ssm_lam_im, ssm_log_dt, ssm_b_re, ssm_b_im, ssm_c_re, ssm_c_im, ssm_d, w_glu_a, w_glu_b, na_rpb, w_out_c, w_o, norm_mix_post, norm_ffn2_pre, norm_ffn2_post, ffn2_w_gate, ffn2_w_up, ffn2_w_down):
    batch, seq, d = x.shape
    depth = w_in.shape[0]
    t = batch * seq
    assert d == D_MODEL and batch % V7X_SUBLANES == 0 and seq % GRID_W == 0
    rows = seq // GRID_W
    assert rows >= NA_ROWS and rows % 2 == 0 and seq % ROW_TILE == 0 and seq % SSM_STEPS == 0

    def row(v):
        return v.reshape(1, -1)

    xt = x.reshape(t, d)
    for i in range(depth):
        xt = _ffn(xt, row(norm_ffn1_pre[i]), row(norm_ffn1_post[i]), ffn1_w_gate[i].astype(BF16),
                  ffn1_w_up[i].astype(BF16), ffn1_w_down[i].astype(BF16))

        w_in_b = w_in[i].astype(BF16)
        ab, p, su, q, k, v = _proj(xt, row(norm_mix_pre[i]), w_in_b[:, :PROJ_COLS])
        ldt = jnp.broadcast_to(ssm_log_dt[i][:, :, None], (2, SSM_GROUPS, SSM_STATE))
        yf, yb = _ssm(su, _state_rows(ssm_lam_re[i]), _state_rows(ssm_lam_im[i]), _state_rows(ldt),
                      _block_diag_in(ssm_b_re[i]), _block_diag_in(ssm_b_im[i]),
                      _block_diag_out(ssm_c_re[i]).astype(BF16), _block_diag_out(ssm_c_im[i]).astype(BF16),
                      batch)
        na = _na(q, k, v, _na_bias_table(na_rpb[i]), batch)
        xt = _merge(xt, ab, p, su, yf, yb, na, row(norm_mix_pre[i]), row(norm_mix_post[i]),
                    w_in_b[:, PROJ_COLS:], conv_w[i], w_out_a[i].astype(BF16), row(ssm_d[i]),
                    w_glu_a[i].astype(BF16), w_glu_b[i].astype(BF16), w_out_c[i].astype(BF16),
                    w_o[i].astype(BF16), seq)

        xt = _ffn(xt, row(norm_ffn2_pre[i]), row(norm_ffn2_post[i]), ffn2_w_gate[i].astype(BF16),
                  ffn2_w_up[i].astype(BF16), ffn2_w_down[i].astype(BF16))
    return xt.reshape(batch, seq, d)
```

```python
import functools

import numpy as np
import jax
import jax.numpy as jnp
from jax import lax
from jax.experimental import pallas as pl
from jax.experimental.pallas import tpu as pltpu

F32 = jnp.float32
BF16 = jnp.bfloat16

D_MODEL = 1024
D_FF = 2816
EPS = 1e-6
CONV_WIDTH = 256
SSM_WIDTH = 256
SSM_GROUP = 16
SSM_GROUPS = SSM_WIDTH // SSM_GROUP
SSM_STATE = 64
SSM_LANES = SSM_GROUPS * SSM_STATE
NA_HEADS = 8
NA_HEAD_DIM = 64
NA_WIDTH = NA_HEADS * NA_HEAD_DIM
NA_ROWS = 8
NA_COLS = 16
GRID_W = 64
NA_KEYS = NA_ROWS * GRID_W
RPB_COLS = 2 * NA_COLS - 1
PROJ_COLS = 3 * CONV_WIDTH + SSM_WIDTH + 3 * NA_WIDTH
MASK_VALUE = -1e30

V7X_VMEM_BYTES = 64 * 1024 * 1024
V7X_LANES = 128
V7X_SUBLANES = 8

NA_PAIRS = NA_WIDTH // V7X_LANES
FFN_CHUNK = 256
ROW_TILE = 512
FFN_ROW_TILE = 1024
SSM_STEPS = 32


def _vmem_limit(nbytes):
    return int(min(V7X_VMEM_BYTES - 8 * 1024 * 1024, max(nbytes, 16 * 1024 * 1024)))


def _rms(x, g):
    ms = jnp.mean(x * x, axis=-1, keepdims=True)
    return x * lax.rsqrt(ms + EPS) * g


def _time_major_rows(tm, tiles_per_seq, width):
    return pl.BlockSpec((tm, width), lambda i: (i % tiles_per_seq, i // tiles_per_seq))


def _resident(shape):
    nd = len(shape)
    return pl.BlockSpec(shape, lambda *_: (0,) * nd, pipeline_mode=pl.Buffered(1))


def _ffn_kernel(x_ref, gpre_ref, gpost_ref, wg_ref, wu_ref, wd_ref, o_ref, acc_ref):
    half = x_ref.shape[0] // 2
    for r in range(2):
        rows = slice(r * half, (r + 1) * half)
        x = x_ref[rows, :]
        h = _rms(x, gpre_ref[...]).astype(BF16)
        for c in range(D_FF // FFN_CHUNK):
            sl = slice(c * FFN_CHUNK, (c + 1) * FFN_CHUNK)
            g = jnp.dot(h, wg_ref[:, sl], preferred_element_type=F32)
            u = jnp.dot(h, wu_ref[:, sl], preferred_element_type=F32)
            a = (g * jax.nn.sigmoid(g) * u).astype(BF16)
            d = jnp.dot(a, wd_ref[sl, :], preferred_element_type=F32)
            if c == 0:
                acc_ref[rows, :] = d
            else:
                acc_ref[rows, :] += d
        o_ref[rows, :] = x + 0.5 * _rms(acc_ref[rows, :], gpost_ref[...])


def _ffn(x, gpre, gpost, wg, wu, wd):
    t = x.shape[0]
    tm = FFN_ROW_TILE
    row = pl.BlockSpec((tm, D_MODEL), lambda i: (i, 0))
    est = 3 * D_MODEL * D_FF * 2 + 5 * tm * D_MODEL * 4 + 8 * tm * FFN_CHUNK * 4 + (4 << 20)
    return pl.pallas_call(
        _ffn_kernel,
        grid=(t // tm,),
        in_specs=[row, _resident((1, D_MODEL)), _resident((1, D_MODEL)),
                  _resident((D_MODEL, D_FF)), _resident((D_MODEL, D_FF)), _resident((D_FF, D_MODEL))],
        out_specs=row,
        out_shape=jax.ShapeDtypeStruct((t, D_MODEL), F32),
        scratch_shapes=[pltpu.VMEM((tm, D_MODEL), F32)],
        compiler_params=pltpu.CompilerParams(
            dimension_semantics=("parallel",), vmem_limit_bytes=_vmem_limit(est)),
        name="ffn",
    )(x, gpre, gpost, wg, wu, wd)


def _proj_kernel(x_ref, g_ref, w_ref, ab_ref, p_ref, su_ref, q_ref, k_ref, v_ref):
    h = _rms(x_ref[...], g_ref[...]).astype(BF16)
    cw = CONV_WIDTH
    ab_ref[...] = jnp.dot(h, w_ref[:, 0:cw], preferred_element_type=F32)
    a_c = jnp.dot(h, w_ref[:, cw:2 * cw], preferred_element_type=F32)
    a_v = jnp.dot(h, w_ref[:, 2 * cw:3 * cw], preferred_element_type=F32)
    p_ref[...] = a_c * a_v
    o = 3 * cw
    su_ref[...] = jnp.dot(h, w_ref[:, o:o + SSM_WIDTH], preferred_element_type=F32)
    o += SSM_WIDTH
    q = jnp.dot(h, w_ref[:, o:o + NA_WIDTH], preferred_element_type=F32)
    q_ref[...] = (q * (NA_HEAD_DIM ** -0.5)).astype(BF16)
    o += NA_WIDTH
    k_ref[...] = jnp.dot(h, w_ref[:, o:o + NA_WIDTH], preferred_element_type=F32).astype(BF16)
    o += NA_WIDTH
    v_ref[...] = jnp.dot(h, w_ref[:, o:o + NA_WIDTH], preferred_element_type=F32).astype(BF16)


def _proj(x, g, w, batch):
    t = x.shape[0]
    tm = ROW_TILE
    seq = t // batch

    def rows(c):
        return pl.BlockSpec((tm, c), lambda i: (i, 0))

    est = w.size * 2 + 2 * tm * (D_MODEL + 3 * CONV_WIDTH + 3 * NA_WIDTH) * 4 + (8 << 20)
    return pl.pallas_call(
        _proj_kernel,
        grid=(t // tm,),
        in_specs=[rows(D_MODEL), _resident((1, D_MODEL)), _resident(w.shape)],
        out_specs=[rows(CONV_WIDTH), rows(CONV_WIDTH), _time_major_rows(tm, seq // tm, SSM_WIDTH),
                   rows(NA_WIDTH), rows(NA_WIDTH), rows(NA_WIDTH)],
        out_shape=[jax.ShapeDtypeStruct((t, CONV_WIDTH), F32), jax.ShapeDtypeStruct((t, CONV_WIDTH), F32),
                   jax.ShapeDtypeStruct((seq, batch * SSM_WIDTH), F32), jax.ShapeDtypeStruct((t, NA_WIDTH), BF16),
                   jax.ShapeDtypeStruct((t, NA_WIDTH), BF16), jax.ShapeDtypeStruct((t, NA_WIDTH), BF16)],
        compiler_params=pltpu.CompilerParams(
            dimension_semantics=("parallel",), vmem_limit_bytes=_vmem_limit(est)),
        name="proj",
    )(x, g, w)


def _ssm_kernel(uf_ref, ub_ref, lre_ref, lim_ref, ldt_ref, bre_ref, bim_ref, cre_ref, cim_ref,
                yf_ref, yb_ref, a_scr, bb_scr, bu_scr, st_scr, u_scr, y_scr, *, batch, steps):
    slabs = SSM_WIDTH // V7X_LANES
    n = SSM_LANES
    halves = batch // V7X_SUBLANES

    @pl.when(pl.program_id(0) == 0)
    def _discretise():
        st_scr[...] = jnp.zeros_like(st_scr)
        for d in range(2):
            lr = jnp.minimum(lre_ref[d], -1e-4)
            li = lim_ref[d]
            dt = jnp.exp(ldt_ref[d])
            mag = jnp.exp(lr * dt)
            a_re = mag * jnp.cos(li * dt)
            a_im = mag * jnp.sin(li * dt)
            den = lr * lr + li * li
            nr, ni = a_re - 1.0, a_im
            f_re = (nr * lr + ni * li) / den
            f_im = (ni * lr - nr * li) / den
            a_scr[d, 0] = a_re
            a_scr[d, 1] = a_im
            fr, fi = f_re[0:1], f_im[0:1]
            br, bi = bre_ref[d], bim_ref[d]
            bb_scr[d, :, 0:n] = (fr * br - fi * bi).astype(BF16)
            bb_scr[d, :, n:2 * n] = (fr * bi + fi * br).astype(BF16)

    def run(d, u_ref, y_ref, reverse):
        for b in range(batch):
            for s in range(slabs):
                lanes = slice(b * SSM_WIDTH + s * V7X_LANES, b * SSM_WIDTH + (s + 1) * V7X_LANES)
                u_scr[d, s, pl.ds(b, steps, stride=batch), :] = u_ref[:, lanes]
        u = jnp.concatenate([u_scr[d, s] for s in range(slabs)], axis=1).astype(BF16)
        bu = bu_scr.at[d]
        bu[...] = jnp.dot(u, bb_scr[d], preferred_element_type=F32)
        a_re = a_scr[d, 0]
        a_im = a_scr[d, 1]

        def body(k, carry):
            t = (steps - 1 - k) if reverse else k
            r0 = pl.multiple_of(t * batch, batch)
            new = []
            for hf in range(halves):
                s_re, s_im = carry[2 * hf], carry[2 * hf + 1]
                rows = pl.ds(r0 + V7X_SUBLANES * hf, V7X_SUBLANES)
                n_re = a_re * s_re - a_im * s_im + bu[rows, 0:n]
                n_im = a_re * s_im + a_im * s_re + bu[rows, n:2 * n]
                bu[rows, 0:n] = n_re
                bu[rows, n:2 * n] = n_im
                new += [n_re, n_im]
            return tuple(new)

        init = tuple(st_scr[d, j] for j in range(2 * halves))
        fin = lax.fori_loop(0, steps, body, init, unroll=True)
        for j in range(2 * halves):
            st_scr[d, j] = fin[j]
        s_re = bu[:, 0:n].astype(BF16)
        s_im = bu[:, n:2 * n].astype(BF16)
        y = (jnp.dot(s_re, cre_ref[d], preferred_element_type=F32)
             - jnp.dot(s_im, cim_ref[d], preferred_element_type=F32))
        for s in range(slabs):
            y_scr[d, s] = y[:, s * V7X_LANES:(s + 1) * V7X_LANES]
        for b in range(batch):
            for s in range(slabs):
                lanes = slice(b * SSM_WIDTH + s * V7X_LANES, b * SSM_WIDTH + (s + 1) * V7X_LANES)
                y_ref[:, lanes] = y_scr[d, s, pl.ds(b, steps, stride=batch), :]

    run(0, uf_ref, yf_ref, False)
    run(1, ub_ref, yb_ref, True)


def _ssm(su, lre, lim, ldt, bre, bim, cre, cim, batch):
    seq = su.shape[0]
    steps = SSM_STEPS
    r = steps * batch
    nc = seq // steps
    n = SSM_LANES
    halves = batch // V7X_SUBLANES
    fwd = pl.BlockSpec((steps, batch * SSM_WIDTH), lambda i: (i, 0))
    bwd = pl.BlockSpec((steps, batch * SSM_WIDTH), lambda i: (nc - 1 - i, 0))
    est = (2 * SSM_WIDTH * n * 4 * 2 + 2 * n * SSM_WIDTH * 2 * 2 + 2 * SSM_WIDTH * 2 * n * 2
           + 4 * r * 2 * n * 4 + 16 * r * SSM_WIDTH * 4 + (8 << 20))
    slab = pltpu.VMEM((2, SSM_WIDTH // V7X_LANES, r, V7X_LANES), F32)
    return pl.pallas_call(
        functools.partial(_ssm_kernel, batch=batch, steps=steps),
        grid=(nc,),
        in_specs=[fwd, bwd] + [_resident((2, V7X_SUBLANES, n))] * 3
                 + [_resident((2, SSM_WIDTH, n))] * 2 + [_resident((2, n, SSM_WIDTH))] * 2,
        out_specs=[fwd, bwd],
        out_shape=[jax.ShapeDtypeStruct((seq, batch * SSM_WIDTH), F32)] * 2,
        scratch_shapes=[pltpu.VMEM((2, 2, V7X_SUBLANES, n), F32),
                        pltpu.VMEM((2, SSM_WIDTH, 2 * n), BF16),
                        pltpu.VMEM((2, r, 2 * n), F32),
                        pltpu.VMEM((2, 2 * halves, V7X_SUBLANES, n), F32), slab, slab],
        compiler_params=pltpu.CompilerParams(
            dimension_semantics=("arbitrary",), vmem_limit_bytes=_vmem_limit(est)),
        name="ssm",
    )(su, su, lre, lim, ldt, bre, bim, cre, cim)


def _na_kernel(q_ref, k_ref, v_ref, bias_ref, o_ref, s_scr, *, rows):
    lane = lax.broadcasted_iota(jnp.int32, (GRID_W, V7X_LANES), 1)
    first = lane < NA_HEAD_DIM
    zero = jnp.zeros((GRID_W, V7X_LANES), BF16)

    def window(r):
        rs = jnp.clip(r - NA_ROWS // 2, 0, rows - NA_ROWS)
        return r - rs, pl.multiple_of(r * GRID_W, GRID_W), pl.multiple_of(rs * GRID_W, GRID_W)

    def scores(r, buf):
        off, q0, k0 = window(r)
        for j in range(NA_PAIRS):
            cols = slice(j * V7X_LANES, (j + 1) * V7X_LANES)
            qp = q_ref[pl.ds(q0, GRID_W), cols]
            q2 = jnp.concatenate([jnp.where(first, qp, zero), jnp.where(first, zero, qp)], axis=0)
            kw = k_ref[pl.ds(k0, NA_KEYS), cols]
            s = lax.dot_general(q2, kw, (((1,), (1,)), ((), ())), preferred_element_type=F32)
            s_scr[buf, j] = s + bias_ref[j * NA_ROWS + off]

    def attend(r, buf):
        _, q0, k0 = window(r)
        for j in range(NA_PAIRS):
            cols = slice(j * V7X_LANES, (j + 1) * V7X_LANES)
            s = s_scr[buf, j]
            p = jnp.exp(s - jnp.max(s, axis=-1, keepdims=True))
            inv = 1.0 / jnp.sum(p, axis=-1, keepdims=True)
            vw = v_ref[pl.ds(k0, NA_KEYS), cols]
            pv = jnp.dot(p.astype(BF16), vw, preferred_element_type=F32) * inv
            o_ref[pl.ds(q0, GRID_W), cols] = jnp.where(first, pv[:GRID_W], pv[GRID_W:]).astype(BF16)

    scores(0, 0)

    def body(rr, carry):
        r = 2 * rr
        scores(r + 1, 1)
        attend(r, 0)
        scores(jnp.minimum(r + 2, rows - 1), 0)
        attend(r + 1, 1)
        return carry

    lax.fori_loop(0, rows // 2, body, 0)


def _na(q, k, v, bias, batch):
    t = q.shape[0]
    seq = t // batch
    rows = seq // GRID_W
    blk = pl.BlockSpec((seq, NA_WIDTH), lambda b: (b, 0))
    est = 8 * seq * NA_WIDTH * 2 + bias.size * 4 + 2 * NA_PAIRS * 2 * GRID_W * NA_KEYS * 4 + (12 << 20)
    return pl.pallas_call(
        functools.partial(_na_kernel, rows=rows),
        grid=(batch,),
        in_specs=[blk, blk, blk, _resident(bias.shape)],
        out_specs=blk,
        out_shape=jax.ShapeDtypeStruct((t, NA_WIDTH), BF16),
        scratch_shapes=[pltpu.VMEM((2, NA_PAIRS, 2 * GRID_W, NA_KEYS), F32)],
        compiler_params=pltpu.CompilerParams(
            dimension_semantics=("parallel",), vmem_limit_bytes=_vmem_limit(est)),
        name="na",
    )(q, k, v, bias)


def _na_bias_table(rpb):
    by_off = jnp.stack([rpb[:, NA_ROWS - 1 - off:2 * NA_ROWS - 1 - off, :] for off in range(NA_ROWS)],
                       axis=1)
    pad = GRID_W - NA_COLS
    ext = jnp.concatenate([jnp.repeat(by_off[..., :1], pad, axis=-1), by_off,
                           jnp.repeat(by_off[..., -1:], pad + 1, axis=-1)], axis=-1)
    lead = ext.shape[:-1]
    skew = jnp.broadcast_to(ext[..., None, :], lead + (GRID_W, 2 * GRID_W)).reshape(lead + (-1,))
    skew = skew[..., :GRID_W * (2 * GRID_W - 1)].reshape(lead + (GRID_W, 2 * GRID_W - 1))
    tab = jnp.swapaxes(skew[..., GRID_W - 1:], 2, 3)
    qc = np.arange(GRID_W)[:, None]
    kc = np.arange(GRID_W)[None, :]
    cs = np.clip(qc - NA_COLS // 2, 0, GRID_W - NA_COLS)
    valid = (kc >= cs) & (kc < cs + NA_COLS)
    tab = jnp.where(valid[None, None, :, None, :], tab, MASK_VALUE)
    tab = tab.reshape(NA_PAIRS, 2, NA_ROWS, GRID_W, NA_KEYS)
    tab = jnp.transpose(tab, (0, 2, 1, 3, 4))
    return tab.reshape(NA_PAIRS * NA_ROWS, 2 * GRID_W, NA_KEYS).astype(F32)


def _merge_kernel(x_ref, ab_ref, p_ref, pp_ref, pn_ref, su_ref, yf_ref, yb_ref, na_ref,
                  gpre_ref, gpost_ref, win_ref, convw_ref, woa_ref, dsk_ref, wga_ref, wgb_ref,
                  woc_ref, wo_ref, o_ref, *, tiles_per_seq):
    i = pl.program_id(0)
    x = x_ref[...]
    tm = x.shape[0]
    h = _rms(x, gpre_ref[...]).astype(BF16)

    p = p_ref[...]
    row = lax.broadcasted_iota(jnp.int32, p.shape, 0)
    at_start = i % tiles_per_seq == 0
    at_end = i % tiles_per_seq == tiles_per_seq - 1
    p_prev = jnp.where(at_start, 0.0, pp_ref[V7X_SUBLANES - 1:V7X_SUBLANES, :])
    p_next = jnp.where(at_end, 0.0, pn_ref[0:1, :])
    before = jnp.where(row == 0, p_prev, pltpu.roll(p, 1, axis=0))
    after = jnp.where(row == tm - 1, p_next, pltpu.roll(p, tm - 1, axis=0))
    cw = convw_ref[...]
    conv = cw[0:1] * before + cw[1:2] * p + cw[2:3] * after
    y_a = jnp.dot((ab_ref[...] * conv).astype(BF16), woa_ref[...], preferred_element_type=F32)
    g_a = jnp.dot(h, win_ref[:, PROJ_COLS:PROJ_COLS + D_MODEL], preferred_element_type=F32)
    mix = jax.nn.sigmoid(g_a) * y_a

    y_s = dsk_ref[...] * su_ref[...] + yf_ref[...] + yb_ref[...]
    z = jax.nn.gelu(y_s).astype(BF16)
    y_b = (jnp.dot(z, wga_ref[...], preferred_element_type=F32)
           * jax.nn.sigmoid(jnp.dot(z, wgb_ref[...], preferred_element_type=F32)))
    g_b = jnp.dot(h, win_ref[:, PROJ_COLS + D_MODEL:PROJ_COLS + 2 * D_MODEL], preferred_element_type=F32)
    mix = mix + jax.nn.sigmoid(g_b) * y_b

    y_c = jnp.dot(na_ref[...], woc_ref[...], preferred_element_type=F32)
    g_c = jnp.dot(h, win_ref[:, PROJ_COLS + 2 * D_MODEL:PROJ_COLS + 3 * D_MODEL], preferred_element_type=F32)
    mix = mix + jax.nn.sigmoid(g_c) * y_c

    y = jnp.dot(mix.astype(BF16), wo_ref[...], preferred_element_type=F32)
    o_ref[...] = x + _rms(y, gpost_ref[...])


def _merge(x, ab, p, su, yf, yb, na, gpre, gpost, win, convw, woa, dsk, wga, wgb, woc, wo):
    t = x.shape[0]
    seq = su.shape[0]
    tm = ROW_TILE
    tmaj = _time_major_rows(tm, seq // tm, SSM_WIDTH)
    per = tm // V7X_SUBLANES
    last = t // V7X_SUBLANES - 1

    def rows(c):
        return pl.BlockSpec((tm, c), lambda i: (i, 0))

    prev = pl.BlockSpec((V7X_SUBLANES, CONV_WIDTH), lambda i: (jnp.maximum(i * per - 1, 0), 0))
    nxt = pl.BlockSpec((V7X_SUBLANES, CONV_WIDTH), lambda i: (jnp.minimum((i + 1) * per, last), 0))
    wbytes = (win.size + CONV_WIDTH * D_MODEL + 2 * SSM_WIDTH * D_MODEL
              + NA_WIDTH * D_MODEL + D_MODEL * D_MODEL) * 2
    est = wbytes + 2 * tm * (2 * D_MODEL + 5 * CONV_WIDTH) * 4 + 10 * tm * D_MODEL * 4 + (8 << 20)
    return pl.pallas_call(
        functools.partial(_merge_kernel, tiles_per_seq=seq // tm),
        grid=(t // tm,),
        in_specs=[rows(D_MODEL), rows(CONV_WIDTH), rows(CONV_WIDTH), prev, nxt, tmaj, tmaj, tmaj,
                  rows(NA_WIDTH),
                  _resident((1, D_MODEL)), _resident((1, D_MODEL)), _resident(win.shape),
                  _resident((3, CONV_WIDTH)), _resident((CONV_WIDTH, D_MODEL)), _resident((1, SSM_WIDTH)),
                  _resident((SSM_WIDTH, D_MODEL)), _resident((SSM_WIDTH, D_MODEL)),
                  _resident((NA_WIDTH, D_MODEL)), _resident((D_MODEL, D_MODEL))],
        out_specs=rows(D_MODEL),
        out_shape=jax.ShapeDtypeStruct((t, D_MODEL), F32),
        compiler_params=pltpu.CompilerParams(
            dimension_semantics=("parallel",), vmem_limit_bytes=_vmem_limit(est)),
        name="merge",
    )(x, ab, p, p, p, su, yf, yb, na, gpre, gpost, win, convw, woa, dsk, wga, wgb, woc, wo)


def _state_rows(a):
    return jnp.broadcast_to(a.reshape(2, 1, SSM_LANES), (2, V7X_SUBLANES, SSM_LANES))


def _block_diag_in(b):
    eye = jnp.eye(SSM_GROUPS, dtype=b.dtype)
    out = jnp.swapaxes(b, 2, 3)[:, :, :, None, :] * eye[None, :, None, :, None]
    return out.reshape(2, SSM_WIDTH, SSM_LANES)


def _block_diag_out(c):
    eye = jnp.eye(SSM_GROUPS, dtype=c.dtype)
    out = jnp.swapaxes(c, 2, 3)[:, :, :, None, :] * eye[None, :, None, :, None]
    return out.reshape(2, SSM_LANES, SSM_WIDTH)


def kernel(x, norm_ffn1_pre, norm_ffn1_post, ffn1_w_gate, ffn1_w_up, ffn1_w_down, norm_mix_pre, w_in, conv_w, w_out_a, ssm_lam_re, ssm_lam_im, ssm_log_dt, ssm_b_re, ssm_b_im, ssm_c_re, ssm_c_im, ssm_d, w_glu_a, w_glu_b, na_rpb, w_out_c, w_o, norm_mix_post, norm_ffn2_pre, norm_ffn2_post, ffn2_w_gate, ffn2_w_up, ffn2_w_down):
    batch, seq, d = x.shape
    depth = w_in.shape[0]
    t = batch * seq
    assert d == D_MODEL and batch % V7X_SUBLANES == 0 and seq % GRID_W == 0
    rows = seq // GRID_W
    assert rows >= NA_ROWS and rows % 2 == 0 and seq % ROW_TILE == 0 and seq % SSM_STEPS == 0
    assert t % FFN_ROW_TILE == 0

    def row(v):
        return v.reshape(1, -1)

    xt = x.reshape(t, d)
    for i in range(depth):
        xt = _ffn(xt, row(norm_ffn1_pre[i]), row(norm_ffn1_post[i]), ffn1_w_gate[i].astype(BF16),
                  ffn1_w_up[i].astype(BF16), ffn1_w_down[i].astype(BF16))

        w_in_b = w_in[i].astype(BF16)
        ab, p, su, q, k, v = _proj(xt, row(norm_mix_pre[i]), w_in_b, batch)
        ldt = jnp.broadcast_to(ssm_log_dt[i][:, :, None], (2, SSM_GROUPS, SSM_STATE))
        yf, yb = _ssm(su, _state_rows(ssm_lam_re[i]), _state_rows(ssm_lam_im[i]), _state_rows(ldt),
                      _block_diag_in(ssm_b_re[i]), _block_diag_in(ssm_b_im[i]),
                      _block_diag_out(ssm_c_re[i]).astype(BF16), _block_diag_out(ssm_c_im[i]).astype(BF16),
                      batch)
        na = _na(q, k, v, _na_bias_table(na_rpb[i]), batch)
        xt = _merge(xt, ab, p, su, yf, yb, na, row(norm_mix_pre[i]), row(norm_mix_post[i]),
                    w_in_b, conv_w[i], w_out_a[i].astype(BF16), row(ssm_d[i]),
                    w_glu_a[i].astype(BF16), w_glu_b[i].astype(BF16), w_out_c[i].astype(BF16),
                    w_o[i].astype(BF16))

        xt = _ffn(xt, row(norm_ffn2_pre[i]), row(norm_ffn2_post[i]), ffn2_w_gate[i].astype(BF16),
                  ffn2_w_up[i].astype(BF16), ffn2_w_down[i].astype(BF16))
    return xt.reshape(batch, seq, d)
```

```python
import functools

import numpy as np
import jax
import jax.numpy as jnp
from jax import lax
from jax.experimental import pallas as pl
from jax.experimental.pallas import tpu as pltpu

F32 = jnp.float32
BF16 = jnp.bfloat16

D_MODEL = 1024
D_FF = 2816
EPS = 1e-6
CONV_WIDTH = 256
SSM_WIDTH = 256
SSM_GROUP = 16
SSM_GROUPS = SSM_WIDTH // SSM_GROUP
SSM_STATE = 64
SSM_LANES = SSM_GROUPS * SSM_STATE
NA_HEADS = 8
NA_HEAD_DIM = 64
NA_WIDTH = NA_HEADS * NA_HEAD_DIM
NA_ROWS = 8
NA_COLS = 16
GRID_W = 64
NA_KEYS = NA_ROWS * GRID_W
RPB_COLS = 2 * NA_COLS - 1
PROJ_COLS = 3 * CONV_WIDTH + SSM_WIDTH + 3 * NA_WIDTH
MASK_VALUE = -1e30

V7X_VMEM_BYTES = 64 * 1024 * 1024
V7X_LANES = 128
V7X_SUBLANES = 8

NA_PAIRS = NA_WIDTH // V7X_LANES
FFN_CHUNK = 256
ROW_TILE = 512
FFN_ROW_TILE = 1024
SSM_STEPS = 32


def _vmem_limit(nbytes):
    return int(min(V7X_VMEM_BYTES - 8 * 1024 * 1024, max(nbytes, 16 * 1024 * 1024)))


def _rms(x, g):
    ms = jnp.mean(x * x, axis=-1, keepdims=True)
    return x * lax.rsqrt(ms + EPS) * g


def _sigmoid(x):
    return 0.5 * jnp.tanh(0.5 * x) + 0.5


def _time_major_rows(tm, tiles_per_seq, width):
    return pl.BlockSpec((tm, width), lambda i: (i % tiles_per_seq, i // tiles_per_seq))


def _resident(shape, layer):
    nd = len(shape)
    return pl.BlockSpec((None,) + tuple(shape), lambda *_: (layer,) + (0,) * nd,
                        pipeline_mode=pl.Buffered(1))


def _ffn_kernel(x_ref, gpre_ref, gpost_ref, wg_ref, wu_ref, wd_ref, o_ref, acc_ref):
    half = x_ref.shape[0] // 2
    for r in range(2):
        rows = slice(r * half, (r + 1) * half)
        x = x_ref[rows, :]
        h = _rms(x, gpre_ref[...]).astype(BF16)
        for c in range(D_FF // FFN_CHUNK):
            sl = slice(c * FFN_CHUNK, (c + 1) * FFN_CHUNK)
            g = jnp.dot(h, wg_ref[:, sl], preferred_element_type=F32)
            u = jnp.dot(h, wu_ref[:, sl], preferred_element_type=F32)
            a = (g * jax.nn.sigmoid(g) * u).astype(BF16)
            d = jnp.dot(a, wd_ref[sl, :], preferred_element_type=F32)
            if c == 0:
                acc_ref[rows, :] = d
            else:
                acc_ref[rows, :] += d
        o_ref[rows, :] = x + 0.5 * _rms(acc_ref[rows, :], gpost_ref[...])


def _ffn(x, gpre, gpost, wg, wu, wd, layer):
    t = x.shape[0]
    tm = FFN_ROW_TILE
    row = pl.BlockSpec((tm, D_MODEL), lambda i: (i, 0))
    est = 3 * D_MODEL * D_FF * 2 + 5 * tm * D_MODEL * 4 + 8 * tm * FFN_CHUNK * 4 + (4 << 20)
    return pl.pallas_call(
        _ffn_kernel,
        grid=(t // tm,),
        in_specs=[row, _resident((1, D_MODEL), layer), _resident((1, D_MODEL), layer),
                  _resident((D_MODEL, D_FF), layer), _resident((D_MODEL, D_FF), layer),
                  _resident((D_FF, D_MODEL), layer)],
        out_specs=row,
        out_shape=jax.ShapeDtypeStruct((t, D_MODEL), F32),
        scratch_shapes=[pltpu.VMEM((tm, D_MODEL), F32)],
        compiler_params=pltpu.CompilerParams(
            dimension_semantics=("parallel",), vmem_limit_bytes=_vmem_limit(est)),
        name="ffn",
    )(x, gpre, gpost, wg, wu, wd)


def _proj_kernel(x_ref, g_ref, w_ref, ab_ref, p_ref, su_ref, q_ref, k_ref, v_ref):
    h = _rms(x_ref[...], g_ref[...]).astype(BF16)
    cw = CONV_WIDTH
    ab_ref[...] = jnp.dot(h, w_ref[:, 0:cw], preferred_element_type=F32)
    a_c = jnp.dot(h, w_ref[:, cw:2 * cw], preferred_element_type=F32)
    a_v = jnp.dot(h, w_ref[:, 2 * cw:3 * cw], preferred_element_type=F32)
    p_ref[...] = a_c * a_v
    o = 3 * cw
    su_ref[...] = jnp.dot(h, w_ref[:, o:o + SSM_WIDTH], preferred_element_type=F32)
    o += SSM_WIDTH
    q = jnp.dot(h, w_ref[:, o:o + NA_WIDTH], preferred_element_type=F32)
    q_ref[...] = (q * (NA_HEAD_DIM ** -0.5)).astype(BF16)
    o += NA_WIDTH
    k_ref[...] = jnp.dot(h, w_ref[:, o:o + NA_WIDTH], preferred_element_type=F32).astype(BF16)
    o += NA_WIDTH
    v_ref[...] = jnp.dot(h, w_ref[:, o:o + NA_WIDTH], preferred_element_type=F32).astype(BF16)


def _proj(x, g, w, batch, layer):
    t = x.shape[0]
    tm = ROW_TILE
    seq = t // batch

    def rows(c):
        return pl.BlockSpec((tm, c), lambda i: (i, 0))

    est = D_MODEL * w.shape[-1] * 2 + 2 * tm * (D_MODEL + 3 * CONV_WIDTH + 3 * NA_WIDTH) * 4 + (8 << 20)
    return pl.pallas_call(
        _proj_kernel,
        grid=(t // tm,),
        in_specs=[rows(D_MODEL), _resident((1, D_MODEL), layer), _resident(w.shape[1:], layer)],
        out_specs=[rows(CONV_WIDTH), rows(CONV_WIDTH), _time_major_rows(tm, seq // tm, SSM_WIDTH),
                   rows(NA_WIDTH), rows(NA_WIDTH), rows(NA_WIDTH)],
        out_shape=[jax.ShapeDtypeStruct((t, CONV_WIDTH), F32), jax.ShapeDtypeStruct((t, CONV_WIDTH), F32),
                   jax.ShapeDtypeStruct((seq, batch * SSM_WIDTH), F32), jax.ShapeDtypeStruct((t, NA_WIDTH), BF16),
                   jax.ShapeDtypeStruct((t, NA_WIDTH), BF16), jax.ShapeDtypeStruct((t, NA_WIDTH), BF16)],
        compiler_params=pltpu.CompilerParams(
            dimension_semantics=("parallel",), vmem_limit_bytes=_vmem_limit(est)),
        name="proj",
    )(x, g, w)


def _ssm_kernel(uf_ref, ub_ref, lre_ref, lim_ref, ldt_ref, bre_ref, bim_ref, cre_ref, cim_ref,
                yf_ref, yb_ref, a_scr, bb_scr, bu_scr, st_scr, u_scr, y_scr, *, batch, steps):
    slabs = SSM_WIDTH // V7X_LANES
    n = SSM_LANES
    halves = batch // V7X_SUBLANES

    @pl.when(pl.program_id(0) == 0)
    def _discretise():
        st_scr[...] = jnp.zeros_like(st_scr)
        for d in range(2):
            lr = jnp.minimum(lre_ref[d], -1e-4)
            li = lim_ref[d]
            dt = jnp.exp(ldt_ref[d])
            mag = jnp.exp(lr * dt)
            a_re = mag * jnp.cos(li * dt)
            a_im = mag * jnp.sin(li * dt)
            den = lr * lr + li * li
            nr, ni = a_re - 1.0, a_im
            f_re = (nr * lr + ni * li) / den
            f_im = (ni * lr - nr * li) / den
            a_scr[d, 0] = a_re
            a_scr[d, 1] = a_im
            fr, fi = f_re[0:1], f_im[0:1]
            br, bi = bre_ref[d], bim_ref[d]
            bb_scr[d, :, 0:n] = (fr * br - fi * bi).astype(BF16)
            bb_scr[d, :, n:2 * n] = (fr * bi + fi * br).astype(BF16)

    def run(d, u_ref, y_ref, reverse):
        for b in range(batch):
            for s in range(slabs):
                lanes = slice(b * SSM_WIDTH + s * V7X_LANES, b * SSM_WIDTH + (s + 1) * V7X_LANES)
                u_scr[d, s, pl.ds(b, steps, stride=batch), :] = u_ref[:, lanes]
        u = jnp.concatenate([u_scr[d, s] for s in range(slabs)], axis=1).astype(BF16)
        bu = bu_scr.at[d]
        bu[...] = jnp.dot(u, bb_scr[d], preferred_element_type=F32)
        a_re = a_scr[d, 0]
        a_im = a_scr[d, 1]

        def body(k, carry):
            t = (steps - 1 - k) if reverse else k
            r0 = pl.multiple_of(t * batch, batch)
            new = []
            for hf in range(halves):
                s_re, s_im = carry[2 * hf], carry[2 * hf + 1]
                rows = pl.ds(r0 + V7X_SUBLANES * hf, V7X_SUBLANES)
                n_re = a_re * s_re - a_im * s_im + bu[rows, 0:n]
                n_im = a_re * s_im + a_im * s_re + bu[rows, n:2 * n]
                bu[rows, 0:n] = n_re
                bu[rows, n:2 * n] = n_im
                new += [n_re, n_im]
            return tuple(new)

        init = tuple(st_scr[d, j] for j in range(2 * halves))
        fin = lax.fori_loop(0, steps, body, init, unroll=True)
        for j in range(2 * halves):
            st_scr[d, j] = fin[j]
        s_re = bu[:, 0:n].astype(BF16)
        s_im = bu[:, n:2 * n].astype(BF16)
        y = (jnp.dot(s_re, cre_ref[d], preferred_element_type=F32)
             - jnp.dot(s_im, cim_ref[d], preferred_element_type=F32))
        for s in range(slabs):
            y_scr[d, s] = y[:, s * V7X_LANES:(s + 1) * V7X_LANES]
        for b in range(batch):
            for s in range(slabs):
                lanes = slice(b * SSM_WIDTH + s * V7X_LANES, b * SSM_WIDTH + (s + 1) * V7X_LANES)
                y_ref[:, lanes] = y_scr[d, s, pl.ds(b, steps, stride=batch), :]

    run(0, uf_ref, yf_ref, False)
    run(1, ub_ref, yb_ref, True)


def _ssm(su, lre, lim, ldt, bre, bim, cre, cim, batch, layer):
    seq = su.shape[0]
    steps = SSM_STEPS
    r = steps * batch
    nc = seq // steps
    n = SSM_LANES
    halves = batch // V7X_SUBLANES
    fwd = pl.BlockSpec((steps, batch * SSM_WIDTH), lambda i: (i, 0))
    bwd = pl.BlockSpec((steps, batch * SSM_WIDTH), lambda i: (nc - 1 - i, 0))
    est = (2 * SSM_WIDTH * n * 4 * 2 + 2 * n * SSM_WIDTH * 2 * 2 + 2 * SSM_WIDTH * 2 * n * 2
           + 4 * r * 2 * n * 4 + 16 * r * SSM_WIDTH * 4 + (8 << 20))
    slab = pltpu.VMEM((2, SSM_WIDTH // V7X_LANES, r, V7X_LANES), F32)
    return pl.pallas_call(
        functools.partial(_ssm_kernel, batch=batch, steps=steps),
        grid=(nc,),
        in_specs=[fwd, bwd] + [_resident((2, V7X_SUBLANES, n), layer)] * 3
                 + [_resident((2, SSM_WIDTH, n), layer)] * 2 + [_resident((2, n, SSM_WIDTH), layer)] * 2,
        out_specs=[fwd, bwd],
        out_shape=[jax.ShapeDtypeStruct((seq, batch * SSM_WIDTH), F32)] * 2,
        scratch_shapes=[pltpu.VMEM((2, 2, V7X_SUBLANES, n), F32),
                        pltpu.VMEM((2, SSM_WIDTH, 2 * n), BF16),
                        pltpu.VMEM((2, r, 2 * n), F32),
                        pltpu.VMEM((2, 2 * halves, V7X_SUBLANES, n), F32), slab, slab],
        compiler_params=pltpu.CompilerParams(
            dimension_semantics=("arbitrary",), vmem_limit_bytes=_vmem_limit(est)),
        name="ssm",
    )(su, su, lre, lim, ldt, bre, bim, cre, cim)


def _na_kernel(q_ref, k_ref, v_ref, bias_ref, o_ref, s_scr, *, rows):
    lane = lax.broadcasted_iota(jnp.int32, (GRID_W, V7X_LANES), 1)
    first = lane < NA_HEAD_DIM
    zero = jnp.zeros((GRID_W, V7X_LANES), BF16)

    def window(r):
        rs = jnp.clip(r - NA_ROWS // 2, 0, rows - NA_ROWS)
        return r - rs, pl.multiple_of(r * GRID_W, GRID_W), pl.multiple_of(rs * GRID_W, GRID_W)

    def scores(r, buf):
        off, q0, k0 = window(r)
        for j in range(NA_PAIRS):
            cols = slice(j * V7X_LANES, (j + 1) * V7X_LANES)
            qp = q_ref[pl.ds(q0, GRID_W), cols]
            q2 = jnp.concatenate([jnp.where(first, qp, zero), jnp.where(first, zero, qp)], axis=0)
            kw = k_ref[pl.ds(k0, NA_KEYS), cols]
            s = lax.dot_general(q2, kw, (((1,), (1,)), ((), ())), preferred_element_type=F32)
            s_scr[buf, j] = s + bias_ref[j * NA_ROWS + off]

    def attend(r, buf):
        _, q0, k0 = window(r)
        for j in range(NA_PAIRS):
            cols = slice(j * V7X_LANES, (j + 1) * V7X_LANES)
            s = s_scr[buf, j]
            p = jnp.exp(s - jnp.max(s, axis=-1, keepdims=True))
            inv = 1.0 / jnp.sum(p, axis=-1, keepdims=True)
            vw = v_ref[pl.ds(k0, NA_KEYS), cols]
            pv = jnp.dot(p.astype(BF16), vw, preferred_element_type=F32) * inv
            o_ref[pl.ds(q0, GRID_W), cols] = jnp.where(first, pv[:GRID_W], pv[GRID_W:]).astype(BF16)

    scores(0, 0)

    def body(rr, carry):
        r = 2 * rr
        scores(r + 1, 1)
        attend(r, 0)
        scores(jnp.minimum(r + 2, rows - 1), 0)
        attend(r + 1, 1)
        return carry

    lax.fori_loop(0, rows // 2, body, 0)


def _na(q, k, v, bias, batch, layer):
    t = q.shape[0]
    seq = t // batch
    rows = seq // GRID_W
    blk = pl.BlockSpec((seq, NA_WIDTH), lambda b: (b, 0))
    est = 8 * seq * NA_WIDTH * 2 + bias[0].size * 4 + 2 * NA_PAIRS * 2 * GRID_W * NA_KEYS * 4 + (12 << 20)
    return pl.pallas_call(
        functools.partial(_na_kernel, rows=rows),
        grid=(batch,),
        in_specs=[blk, blk, blk, _resident(bias.shape[1:], layer)],
        out_specs=blk,
        out_shape=jax.ShapeDtypeStruct((t, NA_WIDTH), BF16),
        scratch_shapes=[pltpu.VMEM((2, NA_PAIRS, 2 * GRID_W, NA_KEYS), F32)],
        compiler_params=pltpu.CompilerParams(
            dimension_semantics=("parallel",), vmem_limit_bytes=_vmem_limit(est)),
        name="na",
    )(q, k, v, bias)


def _na_bias_table(rpb):
    by_off = jnp.stack([rpb[:, NA_ROWS - 1 - off:2 * NA_ROWS - 1 - off, :] for off in range(NA_ROWS)],
                       axis=1)
    pad = GRID_W - NA_COLS
    ext = jnp.concatenate([jnp.repeat(by_off[..., :1], pad, axis=-1), by_off,
                           jnp.repeat(by_off[..., -1:], pad + 1, axis=-1)], axis=-1)
    lead = ext.shape[:-1]
    skew = jnp.broadcast_to(ext[..., None, :], lead + (GRID_W, 2 * GRID_W)).reshape(lead + (-1,))
    skew = skew[..., :GRID_W * (2 * GRID_W - 1)].reshape(lead + (GRID_W, 2 * GRID_W - 1))
    tab = jnp.swapaxes(skew[..., GRID_W - 1:], 2, 3)
    qc = np.arange(GRID_W)[:, None]
    kc = np.arange(GRID_W)[None, :]
    cs = np.clip(qc - NA_COLS // 2, 0, GRID_W - NA_COLS)
    valid = (kc >= cs) & (kc < cs + NA_COLS)
    tab = jnp.where(valid[None, None, :, None, :], tab, MASK_VALUE)
    tab = tab.reshape(NA_PAIRS, 2, NA_ROWS, GRID_W, NA_KEYS)
    tab = jnp.transpose(tab, (0, 2, 1, 3, 4))
    return tab.reshape(NA_PAIRS * NA_ROWS, 2 * GRID_W, NA_KEYS).astype(F32)


def _merge_kernel(x_ref, ab_ref, p_ref, pp_ref, pn_ref, su_ref, yf_ref, yb_ref, na_ref,
                  gpre_ref, gpost_ref, win_ref, convw_ref, woa_ref, dsk_ref, wga_ref, wgb_ref,
                  woc_ref, wo_ref, o_ref, *, tiles_per_seq):
    i = pl.program_id(0)
    x = x_ref[...]
    tm = x.shape[0]
    h = _rms(x, gpre_ref[...]).astype(BF16)

    p = p_ref[...]
    row = lax.broadcasted_iota(jnp.int32, p.shape, 0)
    at_start = i % tiles_per_seq == 0
    at_end = i % tiles_per_seq == tiles_per_seq - 1
    p_prev = jnp.where(at_start, 0.0, pp_ref[V7X_SUBLANES - 1:V7X_SUBLANES, :])
    p_next = jnp.where(at_end, 0.0, pn_ref[0:1, :])
    before = jnp.where(row == 0, p_prev, pltpu.roll(p, 1, axis=0))
    after = jnp.where(row == tm - 1, p_next, pltpu.roll(p, tm - 1, axis=0))
    cw = convw_ref[...]
    conv = cw[0:1] * before + cw[1:2] * p + cw[2:3] * after
    y_a = jnp.dot((ab_ref[...] * conv).astype(BF16), woa_ref[...], preferred_element_type=F32)
    g_a = jnp.dot(h, win_ref[:, PROJ_COLS:PROJ_COLS + D_MODEL], preferred_element_type=F32)
    mix = _sigmoid(g_a) * y_a

    y_s = dsk_ref[...] * su_ref[...] + yf_ref[...] + yb_ref[...]
    z = jax.nn.gelu(y_s).astype(BF16)
    y_b = (jnp.dot(z, wga_ref[...], preferred_element_type=F32)
           * _sigmoid(jnp.dot(z, wgb_ref[...], preferred_element_type=F32)))
    g_b = jnp.dot(h, win_ref[:, PROJ_COLS + D_MODEL:PROJ_COLS + 2 * D_MODEL], preferred_element_type=F32)
    mix = mix + _sigmoid(g_b) * y_b

    y_c = jnp.dot(na_ref[...], woc_ref[...], preferred_element_type=F32)
    g_c = jnp.dot(h, win_ref[:, PROJ_COLS + 2 * D_MODEL:PROJ_COLS + 3 * D_MODEL], preferred_element_type=F32)
    mix = mix + _sigmoid(g_c) * y_c

    y = jnp.dot(mix.astype(BF16), wo_ref[...], preferred_element_type=F32)
    o_ref[...] = x + _rms(y, gpost_ref[...])


def _merge(x, ab, p, su, yf, yb, na, gpre, gpost, win, convw, woa, dsk, wga, wgb, woc, wo, layer):
    t = x.shape[0]
    seq = su.shape[0]
    tm = ROW_TILE
    tmaj = _time_major_rows(tm, seq // tm, SSM_WIDTH)
    per = tm // V7X_SUBLANES
    last = t // V7X_SUBLANES - 1

    def rows(c):
        return pl.BlockSpec((tm, c), lambda i: (i, 0))

    prev = pl.BlockSpec((V7X_SUBLANES, CONV_WIDTH), lambda i: (jnp.maximum(i * per - 1, 0), 0))
    nxt = pl.BlockSpec((V7X_SUBLANES, CONV_WIDTH), lambda i: (jnp.minimum((i + 1) * per, last), 0))
    wbytes = (D_MODEL * win.shape[-1] + CONV_WIDTH * D_MODEL + 2 * SSM_WIDTH * D_MODEL
              + NA_WIDTH * D_MODEL + D_MODEL * D_MODEL) * 2
    est = wbytes + 2 * tm * (2 * D_MODEL + 5 * CONV_WIDTH) * 4 + 10 * tm * D_MODEL * 4 + (8 << 20)
    return pl.pallas_call(
        functools.partial(_merge_kernel, tiles_per_seq=seq // tm),
        grid=(t // tm,),
        in_specs=[rows(D_MODEL), rows(CONV_WIDTH), rows(CONV_WIDTH), prev, nxt, tmaj, tmaj, tmaj,
                  rows(NA_WIDTH),
                  _resident((1, D_MODEL), layer), _resident((1, D_MODEL), layer),
                  _resident(win.shape[1:], layer), _resident((3, CONV_WIDTH), layer),
                  _resident((CONV_WIDTH, D_MODEL), layer), _resident((1, SSM_WIDTH), layer),
                  _resident((SSM_WIDTH, D_MODEL), layer), _resident((SSM_WIDTH, D_MODEL), layer),
                  _resident((NA_WIDTH, D_MODEL), layer), _resident((D_MODEL, D_MODEL), layer)],
        out_specs=rows(D_MODEL),
        out_shape=jax.ShapeDtypeStruct((t, D_MODEL), F32),
        compiler_params=pltpu.CompilerParams(
            dimension_semantics=("parallel",), vmem_limit_bytes=_vmem_limit(est)),
        name="merge",
    )(x, ab, p, p, p, su, yf, yb, na, gpre, gpost, win, convw, woa, dsk, wga, wgb, woc, wo)


def _state_rows(a):
    lead = a.shape[:-2]
    return jnp.broadcast_to(a.reshape(lead + (1, SSM_LANES)), lead + (V7X_SUBLANES, SSM_LANES))


def _block_diag_in(b):
    eye = jnp.eye(SSM_GROUPS, dtype=b.dtype)
    out = jnp.swapaxes(b, -1, -2)[..., :, :, None, :] * eye[:, None, :, None]
    return out.reshape(b.shape[:-3] + (SSM_WIDTH, SSM_LANES))


def _block_diag_out(c):
    eye = jnp.eye(SSM_GROUPS, dtype=c.dtype)
    out = jnp.swapaxes(c, -1, -2)[..., :, :, None, :] * eye[:, None, :, None]
    return out.reshape(c.shape[:-3] + (SSM_LANES, SSM_WIDTH))


def _bf16(w):
    return w.astype(BF16)


def kernel(x, norm_ffn1_pre, norm_ffn1_post, ffn1_w_gate, ffn1_w_up, ffn1_w_down, norm_mix_pre, w_in, conv_w, w_out_a, ssm_lam_re, ssm_lam_im, ssm_log_dt, ssm_b_re, ssm_b_im, ssm_c_re, ssm_c_im, ssm_d, w_glu_a, w_glu_b, na_rpb, w_out_c, w_o, norm_mix_post, norm_ffn2_pre, norm_ffn2_post, ffn2_w_gate, ffn2_w_up, ffn2_w_down):
    batch, seq, d = x.shape
    depth = w_in.shape[0]
    t = batch * seq
    assert d == D_MODEL and batch % V7X_SUBLANES == 0 and seq % GRID_W == 0
    rows = seq // GRID_W
    assert rows >= NA_ROWS and rows % 2 == 0 and seq % ROW_TILE == 0 and seq % SSM_STEPS == 0
    assert t % FFN_ROW_TILE == 0

    def rows_of(v):
        return v.reshape(depth, 1, -1)

    ffn1 = (rows_of(norm_ffn1_pre), rows_of(norm_ffn1_post), _bf16(ffn1_w_gate), _bf16(ffn1_w_up),
            _bf16(ffn1_w_down))
    ffn2 = (rows_of(norm_ffn2_pre), rows_of(norm_ffn2_post), _bf16(ffn2_w_gate), _bf16(ffn2_w_up),
            _bf16(ffn2_w_down))
    g_mix_pre, g_mix_post, w_in_b = rows_of(norm_mix_pre), rows_of(norm_mix_post), _bf16(w_in)
    ldt = jnp.broadcast_to(ssm_log_dt[..., None], ssm_lam_re.shape)
    ssm_params = (_state_rows(ssm_lam_re), _state_rows(ssm_lam_im), _state_rows(ldt),
                  _block_diag_in(ssm_b_re), _block_diag_in(ssm_b_im),
                  _bf16(_block_diag_out(ssm_c_re)), _bf16(_block_diag_out(ssm_c_im)))
    na_bias = jax.vmap(_na_bias_table)(na_rpb)
    merge_params = (g_mix_pre, g_mix_post, w_in_b, conv_w, _bf16(w_out_a), rows_of(ssm_d),
                    _bf16(w_glu_a), _bf16(w_glu_b), _bf16(w_out_c), _bf16(w_o))

    xt = x.reshape(t, d)
    for i in range(depth):
        xt = _ffn(xt, *ffn1, i)
        ab, p, su, q, k, v = _proj(xt, g_mix_pre, w_in_b, batch, i)
        yf, yb = _ssm(su, *ssm_params, batch, i)
        na = _na(q, k, v, na_bias, batch, i)
        xt = _merge(xt, ab, p, su, yf, yb, na, *merge_params, i)
        xt = _ffn(xt, *ffn2, i)
    return xt.reshape(batch, seq, d)
```

```python
import functools

import jax
import jax.numpy as jnp
from jax import lax
from jax.experimental import pallas as pl
from jax.experimental.pallas import tpu as pltpu

F32 = jnp.float32
BF16 = jnp.bfloat16

D_MODEL = 1024
D_FF = 2816
EPS = 1e-6
CONV_WIDTH = 256
SSM_WIDTH = 256
SSM_GROUP = 16
SSM_GROUPS = SSM_WIDTH // SSM_GROUP
SSM_STATE = 64
SSM_LANES = SSM_GROUPS * SSM_STATE
NA_HEADS = 8
NA_HEAD_DIM = 64
NA_WIDTH = NA_HEADS * NA_HEAD_DIM
NA_ROWS = 8
NA_COLS = 16
GRID_W = 64
NA_KEYS = NA_ROWS * GRID_W
RPB_COLS = 2 * NA_COLS - 1
PROJ_COLS = 3 * CONV_WIDTH + SSM_WIDTH + 3 * NA_WIDTH
MASK_VALUE = -1e30

V7X_VMEM_BYTES = 64 * 1024 * 1024
V7X_LANES = 128
V7X_SUBLANES = 8

NA_PAIRS = NA_WIDTH // V7X_LANES
FFN_CHUNK = 256
ROW_TILE = 512
FFN_ROW_TILE = 1024
SSM_STEPS = 32


def _vmem_limit(nbytes):
    return int(min(V7X_VMEM_BYTES - 8 * 1024 * 1024, max(nbytes, 16 * 1024 * 1024)))


def _rms(x, g):
    ms = jnp.mean(x * x, axis=-1, keepdims=True)
    return x * lax.rsqrt(ms + EPS) * g


def _sigmoid(x):
    return 0.5 * jnp.tanh(0.5 * x) + 0.5


def _time_major_rows(tm, tiles_per_seq, width):
    return pl.BlockSpec((tm, width), lambda i: (i % tiles_per_seq, i // tiles_per_seq))


def _resident(shape, layer):
    nd = len(shape)
    return pl.BlockSpec((None,) + tuple(shape), lambda *_: (layer,) + (0,) * nd,
                        pipeline_mode=pl.Buffered(1))


def _ffn_kernel(x_ref, gpre_ref, gpost_ref, wg_ref, wu_ref, wd_ref, o_ref, acc_ref):
    half = x_ref.shape[0] // 2
    for r in range(2):
        rows = slice(r * half, (r + 1) * half)
        x = x_ref[rows, :]
        h = _rms(x, gpre_ref[...]).astype(BF16)
        for c in range(D_FF // FFN_CHUNK):
            sl = slice(c * FFN_CHUNK, (c + 1) * FFN_CHUNK)
            g = jnp.dot(h, wg_ref[:, sl], preferred_element_type=F32)
            u = jnp.dot(h, wu_ref[:, sl], preferred_element_type=F32)
            a = (g * jax.nn.sigmoid(g) * u).astype(BF16)
            d = jnp.dot(a, wd_ref[sl, :], preferred_element_type=F32)
            if c == 0:
                acc_ref[rows, :] = d
            else:
                acc_ref[rows, :] += d
        o_ref[rows, :] = x + 0.5 * _rms(acc_ref[rows, :], gpost_ref[...])


def _ffn(x, gpre, gpost, wg, wu, wd, layer):
    t = x.shape[0]
    tm = FFN_ROW_TILE
    row = pl.BlockSpec((tm, D_MODEL), lambda i: (i, 0))
    est = 3 * D_MODEL * D_FF * 2 + 5 * tm * D_MODEL * 4 + 8 * tm * FFN_CHUNK * 4 + (4 << 20)
    return pl.pallas_call(
        _ffn_kernel,
        grid=(t // tm,),
        in_specs=[row, _resident((1, D_MODEL), layer), _resident((1, D_MODEL), layer),
                  _resident((D_MODEL, D_FF), layer), _resident((D_MODEL, D_FF), layer),
                  _resident((D_FF, D_MODEL), layer)],
        out_specs=row,
        out_shape=jax.ShapeDtypeStruct((t, D_MODEL), F32),
        scratch_shapes=[pltpu.VMEM((tm, D_MODEL), F32)],
        compiler_params=pltpu.CompilerParams(
            dimension_semantics=("parallel",), vmem_limit_bytes=_vmem_limit(est)),
        name="ffn",
    )(x, gpre, gpost, wg, wu, wd)


def _proj_kernel(x_ref, g_ref, w_ref, ab_ref, p_ref, su_ref, q_ref, k_ref, v_ref):
    h = _rms(x_ref[...], g_ref[...]).astype(BF16)
    cw = CONV_WIDTH
    ab_ref[...] = jnp.dot(h, w_ref[:, 0:cw], preferred_element_type=F32)
    a_c = jnp.dot(h, w_ref[:, cw:2 * cw], preferred_element_type=F32)
    a_v = jnp.dot(h, w_ref[:, 2 * cw:3 * cw], preferred_element_type=F32)
    p_ref[...] = a_c * a_v
    o = 3 * cw
    su_ref[...] = jnp.dot(h, w_ref[:, o:o + SSM_WIDTH], preferred_element_type=F32)
    o += SSM_WIDTH
    q = jnp.dot(h, w_ref[:, o:o + NA_WIDTH], preferred_element_type=F32)
    q_ref[...] = (q * (NA_HEAD_DIM ** -0.5)).astype(BF16)
    o += NA_WIDTH
    k_ref[...] = jnp.dot(h, w_ref[:, o:o + NA_WIDTH], preferred_element_type=F32).astype(BF16)
    o += NA_WIDTH
    v_ref[...] = jnp.dot(h, w_ref[:, o:o + NA_WIDTH], preferred_element_type=F32).astype(BF16)


def _proj(x, g, w, batch, layer):
    t = x.shape[0]
    tm = ROW_TILE
    seq = t // batch

    def rows(c):
        return pl.BlockSpec((tm, c), lambda i: (i, 0))

    est = D_MODEL * w.shape[-1] * 2 + 2 * tm * (D_MODEL + 3 * CONV_WIDTH + 3 * NA_WIDTH) * 4 + (8 << 20)
    return pl.pallas_call(
        _proj_kernel,
        grid=(t // tm,),
        in_specs=[rows(D_MODEL), _resident((1, D_MODEL), layer), _resident(w.shape[1:], layer)],
        out_specs=[rows(CONV_WIDTH), rows(CONV_WIDTH), _time_major_rows(tm, seq // tm, SSM_WIDTH),
                   rows(NA_WIDTH), rows(NA_WIDTH), rows(NA_WIDTH)],
        out_shape=[jax.ShapeDtypeStruct((t, CONV_WIDTH), F32), jax.ShapeDtypeStruct((t, CONV_WIDTH), F32),
                   jax.ShapeDtypeStruct((seq, batch * SSM_WIDTH), F32), jax.ShapeDtypeStruct((t, NA_WIDTH), BF16),
                   jax.ShapeDtypeStruct((t, NA_WIDTH), BF16), jax.ShapeDtypeStruct((t, NA_WIDTH), BF16)],
        compiler_params=pltpu.CompilerParams(
            dimension_semantics=("parallel",), vmem_limit_bytes=_vmem_limit(est)),
        name="proj",
    )(x, g, w)


def _ssm_kernel(uf_ref, ub_ref, lre_ref, lim_ref, ldt_ref, bre_ref, bim_ref, cre_ref, cim_ref,
                yf_ref, yb_ref, a_scr, bb_scr, bu_scr, st_scr, u_scr, y_scr, *, batch, steps):
    slabs = SSM_WIDTH // V7X_LANES
    n = SSM_LANES
    halves = batch // V7X_SUBLANES

    @pl.when(pl.program_id(0) == 0)
    def _discretise():
        st_scr[...] = jnp.zeros_like(st_scr)
        for d in range(2):
            lr = jnp.minimum(lre_ref[d], -1e-4)
            li = lim_ref[d]
            dt = jnp.exp(ldt_ref[d])
            mag = jnp.exp(lr * dt)
            a_re = mag * jnp.cos(li * dt)
            a_im = mag * jnp.sin(li * dt)
            den = lr * lr + li * li
            nr, ni = a_re - 1.0, a_im
            f_re = (nr * lr + ni * li) / den
            f_im = (ni * lr - nr * li) / den
            a_scr[d, 0] = a_re
            a_scr[d, 1] = a_im
            fr, fi = f_re[0:1], f_im[0:1]
            br, bi = bre_ref[d], bim_ref[d]
            bb_scr[d, :, 0:n] = (fr * br - fi * bi).astype(BF16)
            bb_scr[d, :, n:2 * n] = (fr * bi + fi * br).astype(BF16)

    def run(d, u_ref, y_ref, reverse):
        for b in range(batch):
            for s in range(slabs):
                lanes = slice(b * SSM_WIDTH + s * V7X_LANES, b * SSM_WIDTH + (s + 1) * V7X_LANES)
                u_scr[d, s, pl.ds(b, steps, stride=batch), :] = u_ref[:, lanes]
        u = jnp.concatenate([u_scr[d, s] for s in range(slabs)], axis=1).astype(BF16)
        bu = bu_scr.at[d]
        bu[...] = jnp.dot(u, bb_scr[d], preferred_element_type=F32)
        a_re = a_scr[d, 0]
        a_im = a_scr[d, 1]

        def body(k, carry):
            t = (steps - 1 - k) if reverse else k
            r0 = pl.multiple_of(t * batch, batch)
            new = []
            for hf in range(halves):
                s_re, s_im = carry[2 * hf], carry[2 * hf + 1]
                rows = pl.ds(r0 + V7X_SUBLANES * hf, V7X_SUBLANES)
                n_re = a_re * s_re - a_im * s_im + bu[rows, 0:n]
                n_im = a_re * s_im + a_im * s_re + bu[rows, n:2 * n]
                bu[rows, 0:n] = n_re
                bu[rows, n:2 * n] = n_im
                new += [n_re, n_im]
            return tuple(new)

        init = tuple(st_scr[d, j] for j in range(2 * halves))
        fin = lax.fori_loop(0, steps, body, init, unroll=True)
        for j in range(2 * halves):
            st_scr[d, j] = fin[j]
        s_re = bu[:, 0:n].astype(BF16)
        s_im = bu[:, n:2 * n].astype(BF16)
        y = (jnp.dot(s_re, cre_ref[d], preferred_element_type=F32)
             - jnp.dot(s_im, cim_ref[d], preferred_element_type=F32))
        for s in range(slabs):
            y_scr[d, s] = y[:, s * V7X_LANES:(s + 1) * V7X_LANES]
        for b in range(batch):
            for s in range(slabs):
                lanes = slice(b * SSM_WIDTH + s * V7X_LANES, b * SSM_WIDTH + (s + 1) * V7X_LANES)
                y_ref[:, lanes] = y_scr[d, s, pl.ds(b, steps, stride=batch), :]

    run(0, uf_ref, yf_ref, False)
    run(1, ub_ref, yb_ref, True)


def _ssm(su, lre, lim, ldt, bre, bim, cre, cim, batch, layer):
    seq = su.shape[0]
    steps = SSM_STEPS
    r = steps * batch
    nc = seq // steps
    n = SSM_LANES
    halves = batch // V7X_SUBLANES
    fwd = pl.BlockSpec((steps, batch * SSM_WIDTH), lambda i: (i, 0))
    bwd = pl.BlockSpec((steps, batch * SSM_WIDTH), lambda i: (nc - 1 - i, 0))
    est = (2 * SSM_WIDTH * n * 4 * 2 + 2 * n * SSM_WIDTH * 2 * 2 + 2 * SSM_WIDTH * 2 * n * 2
           + 4 * r * 2 * n * 4 + 16 * r * SSM_WIDTH * 4 + (8 << 20))
    slab = pltpu.VMEM((2, SSM_WIDTH // V7X_LANES, r, V7X_LANES), F32)
    return pl.pallas_call(
        functools.partial(_ssm_kernel, batch=batch, steps=steps),
        grid=(nc,),
        in_specs=[fwd, bwd] + [_resident((2, V7X_SUBLANES, n), layer)] * 3
                 + [_resident((2, SSM_WIDTH, n), layer)] * 2 + [_resident((2, n, SSM_WIDTH), layer)] * 2,
        out_specs=[fwd, bwd],
        out_shape=[jax.ShapeDtypeStruct((seq, batch * SSM_WIDTH), F32)] * 2,
        scratch_shapes=[pltpu.VMEM((2, 2, V7X_SUBLANES, n), F32),
                        pltpu.VMEM((2, SSM_WIDTH, 2 * n), BF16),
                        pltpu.VMEM((2, r, 2 * n), F32),
                        pltpu.VMEM((2, 2 * halves, V7X_SUBLANES, n), F32), slab, slab],
        compiler_params=pltpu.CompilerParams(
            dimension_semantics=("arbitrary",), vmem_limit_bytes=_vmem_limit(est)),
        name="ssm",
    )(su, su, lre, lim, ldt, bre, bim, cre, cim)


def _na_kernel(q_ref, k_ref, v_ref, rpb_ref, o_ref, s_scr, bias_scr, *, rows):
    lane = lax.broadcasted_iota(jnp.int32, (GRID_W, V7X_LANES), 1)
    first = lane < NA_HEAD_DIM
    zero = jnp.zeros((GRID_W, V7X_LANES), BF16)

    @pl.when(pl.program_id(0) == 0)
    def _build_bias():
        qc = lax.broadcasted_iota(jnp.int32, (GRID_W, V7X_LANES), 0)
        kc = jnp.bitwise_and(lane, GRID_W - 1)
        start = jnp.clip(qc - NA_COLS // 2, 0, GRID_W - NA_COLS)
        valid = jnp.logical_and(kc >= start, kc < start + NA_COLS)
        for h in range(NA_HEADS):
            tiles = []
            for dr in range(2 * NA_ROWS - 2):
                x0 = jnp.broadcast_to(rpb_ref[h, dr:dr + 1, :], (GRID_W, V7X_LANES))
                x1 = jnp.broadcast_to(rpb_ref[h, dr + 1:dr + 2, :], (GRID_W, V7X_LANES))
                t0 = pltpu.roll(x0, GRID_W + 1, axis=1, stride=1, stride_axis=0)
                t1 = pltpu.roll(x1, 1, axis=1, stride=1, stride_axis=0)
                tiles.append(jnp.where(valid, jnp.where(first, t0, t1), MASK_VALUE))
            j, hh = divmod(h, 2)
            for off in range(NA_ROWS):
                for m in range(NA_KEYS // V7X_LANES):
                    bias_scr[j * NA_ROWS + off, hh * GRID_W:(hh + 1) * GRID_W,
                             m * V7X_LANES:(m + 1) * V7X_LANES] = tiles[2 * m - off + NA_ROWS - 1]

    def window(r):
        rs = jnp.clip(r - NA_ROWS // 2, 0, rows - NA_ROWS)
        return r - rs, pl.multiple_of(r * GRID_W, GRID_W), pl.multiple_of(rs * GRID_W, GRID_W)

    def scores(r, buf):
        off, q0, k0 = window(r)
        for j in range(NA_PAIRS):
            cols = slice(j * V7X_LANES, (j + 1) * V7X_LANES)
            qp = q_ref[pl.ds(q0, GRID_W), cols]
            q2 = jnp.concatenate([jnp.where(first, qp, zero), jnp.where(first, zero, qp)], axis=0)
            kw = k_ref[pl.ds(k0, NA_KEYS), cols]
            s = lax.dot_general(q2, kw, (((1,), (1,)), ((), ())), preferred_element_type=F32)
            s_scr[buf, j] = s + bias_scr[j * NA_ROWS + off]

    def attend(r, buf):
        _, q0, k0 = window(r)
        for j in range(NA_PAIRS):
            cols = slice(j * V7X_LANES, (j + 1) * V7X_LANES)
            s = s_scr[buf, j]
            p = jnp.exp(s - jnp.max(s, axis=-1, keepdims=True))
            inv = 1.0 / jnp.sum(p, axis=-1, keepdims=True)
            vw = v_ref[pl.ds(k0, NA_KEYS), cols]
            pv = jnp.dot(p.astype(BF16), vw, preferred_element_type=F32) * inv
            o_ref[pl.ds(q0, GRID_W), cols] = jnp.where(first, pv[:GRID_W], pv[GRID_W:]).astype(BF16)

    scores(0, 0)

    def body(rr, carry):
        r = 2 * rr
        scores(r + 1, 1)
        attend(r, 0)
        scores(jnp.minimum(r + 2, rows - 1), 0)
        attend(r + 1, 1)
        return carry

    lax.fori_loop(0, rows // 2, body, 0)


def _na(q, k, v, rpb_ext, batch, layer):
    t = q.shape[0]
    seq = t // batch
    rows = seq // GRID_W
    blk = pl.BlockSpec((seq, NA_WIDTH), lambda b: (b, 0))
    table = NA_PAIRS * NA_ROWS * 2 * GRID_W * NA_KEYS * 4
    est = 8 * seq * NA_WIDTH * 2 + table + 2 * NA_PAIRS * 2 * GRID_W * NA_KEYS * 4 + (12 << 20)
    return pl.pallas_call(
        functools.partial(_na_kernel, rows=rows),
        grid=(batch,),
        in_specs=[blk, blk, blk, _resident(rpb_ext.shape[1:], layer)],
        out_specs=blk,
        out_shape=jax.ShapeDtypeStruct((t, NA_WIDTH), BF16),
        scratch_shapes=[pltpu.VMEM((2, NA_PAIRS, 2 * GRID_W, NA_KEYS), F32),
                        pltpu.VMEM((NA_PAIRS * NA_ROWS, 2 * GRID_W, NA_KEYS), F32)],
        compiler_params=pltpu.CompilerParams(
            dimension_semantics=("arbitrary",), vmem_limit_bytes=_vmem_limit(est)),
        name="na",
    )(q, k, v, rpb_ext)


def _rpb_by_offset(rpb):
    pad = GRID_W - NA_COLS
    ext = jnp.concatenate([jnp.repeat(rpb[..., :1], pad, axis=-1), rpb,
                           jnp.repeat(rpb[..., -1:], pad + 1, axis=-1)], axis=-1)
    return jnp.pad(ext, ((0, 0), (0, 0), (0, 1), (0, 0))).astype(F32)


def _merge_kernel(x_ref, ab_ref, p_ref, pp_ref, pn_ref, su_ref, yf_ref, yb_ref, na_ref,
                  gpre_ref, gpost_ref, win_ref, convw_ref, woa_ref, dsk_ref, wga_ref, wgb_ref,
                  woc_ref, wo_ref, o_ref, *, tiles_per_seq):
    i = pl.program_id(0)
    x = x_ref[...]
    tm = x.shape[0]
    h = _rms(x, gpre_ref[...]).astype(BF16)

    p = p_ref[...]
    row = lax.broadcasted_iota(jnp.int32, p.shape, 0)
    at_start = i % tiles_per_seq == 0
    at_end = i % tiles_per_seq == tiles_per_seq - 1
    p_prev = jnp.where(at_start, 0.0, pp_ref[V7X_SUBLANES - 1:V7X_SUBLANES, :])
    p_next = jnp.where(at_end, 0.0, pn_ref[0:1, :])
    before = jnp.where(row == 0, p_prev, pltpu.roll(p, 1, axis=0))
    after = jnp.where(row == tm - 1, p_next, pltpu.roll(p, tm - 1, axis=0))
    cw = convw_ref[...]
    conv = cw[0:1] * before + cw[1:2] * p + cw[2:3] * after
    y_a = jnp.dot((ab_ref[...] * conv).astype(BF16), woa_ref[...], preferred_element_type=F32)
    g_a = jnp.dot(h, win_ref[:, PROJ_COLS:PROJ_COLS + D_MODEL], preferred_element_type=F32)
    mix = _sigmoid(g_a) * y_a

    y_s = dsk_ref[...] * su_ref[...] + yf_ref[...] + yb_ref[...]
    z = jax.nn.gelu(y_s).astype(BF16)
    y_b = (jnp.dot(z, wga_ref[...], preferred_element_type=F32)
           * _sigmoid(jnp.dot(z, wgb_ref[...], preferred_element_type=F32)))
    g_b = jnp.dot(h, win_ref[:, PROJ_COLS + D_MODEL:PROJ_COLS + 2 * D_MODEL], preferred_element_type=F32)
    mix = mix + _sigmoid(g_b) * y_b

    y_c = jnp.dot(na_ref[...], woc_ref[...], preferred_element_type=F32)
    g_c = jnp.dot(h, win_ref[:, PROJ_COLS + 2 * D_MODEL:PROJ_COLS + 3 * D_MODEL], preferred_element_type=F32)
    mix = mix + _sigmoid(g_c) * y_c

    y = jnp.dot(mix.astype(BF16), wo_ref[...], preferred_element_type=F32)
    o_ref[...] = x + _rms(y, gpost_ref[...])


def _merge(x, ab, p, su, yf, yb, na, gpre, gpost, win, convw, woa, dsk, wga, wgb, woc, wo, layer):
    t = x.shape[0]
    seq = su.shape[0]
    tm = ROW_TILE
    tmaj = _time_major_rows(tm, seq // tm, SSM_WIDTH)
    per = tm // V7X_SUBLANES
    last = t // V7X_SUBLANES - 1

    def rows(c):
        return pl.BlockSpec((tm, c), lambda i: (i, 0))

    prev = pl.BlockSpec((V7X_SUBLANES, CONV_WIDTH), lambda i: (jnp.maximum(i * per - 1, 0), 0))
    nxt = pl.BlockSpec((V7X_SUBLANES, CONV_WIDTH), lambda i: (jnp.minimum((i + 1) * per, last), 0))
    wbytes = (D_MODEL * win.shape[-1] + CONV_WIDTH * D_MODEL + 2 * SSM_WIDTH * D_MODEL
              + NA_WIDTH * D_MODEL + D_MODEL * D_MODEL) * 2
    est = wbytes + 2 * tm * (2 * D_MODEL + 5 * CONV_WIDTH) * 4 + 10 * tm * D_MODEL * 4 + (8 << 20)
    return pl.pallas_call(
        functools.partial(_merge_kernel, tiles_per_seq=seq // tm),
        grid=(t // tm,),
        in_specs=[rows(D_MODEL), rows(CONV_WIDTH), rows(CONV_WIDTH), prev, nxt, tmaj, tmaj, tmaj,
                  rows(NA_WIDTH),
                  _resident((1, D_MODEL), layer), _resident((1, D_MODEL), layer),
                  _resident(win.shape[1:], layer), _resident((3, CONV_WIDTH), layer),
                  _resident((CONV_WIDTH, D_MODEL), layer), _resident((1, SSM_WIDTH), layer),
                  _resident((SSM_WIDTH, D_MODEL), layer), _resident((SSM_WIDTH, D_MODEL), layer),
                  _resident((NA_WIDTH, D_MODEL), layer), _resident((D_MODEL, D_MODEL), layer)],
        out_specs=rows(D_MODEL),
        out_shape=jax.ShapeDtypeStruct((t, D_MODEL), F32),
        compiler_params=pltpu.CompilerParams(
            dimension_semantics=("parallel",), vmem_limit_bytes=_vmem_limit(est)),
        name="merge",
    )(x, ab, p, p, p, su, yf, yb, na, gpre, gpost, win, convw, woa, dsk, wga, wgb, woc, wo)


def _state_rows(a):
    lead = a.shape[:-2]
    return jnp.broadcast_to(a.reshape(lead + (1, SSM_LANES)), lead + (V7X_SUBLANES, SSM_LANES))


def _block_diag_in(b):
    eye = jnp.eye(SSM_GROUPS, dtype=b.dtype)
    out = jnp.swapaxes(b, -1, -2)[..., :, :, None, :] * eye[:, None, :, None]
    return out.reshape(b.shape[:-3] + (SSM_WIDTH, SSM_LANES))


def _block_diag_out(c):
    eye = jnp.eye(SSM_GROUPS, dtype=c.dtype)
    out = jnp.swapaxes(c, -1, -2)[..., :, :, None, :] * eye[:, None, :, None]
    return out.reshape(c.shape[:-3] + (SSM_LANES, SSM_WIDTH))


def _bf16(w):
    return w.astype(BF16)


def kernel(x, norm_ffn1_pre, norm_ffn1_post, ffn1_w_gate, ffn1_w_up, ffn1_w_down, norm_mix_pre, w_in, conv_w, w_out_a, ssm_lam_re, ssm_lam_im, ssm_log_dt, ssm_b_re, ssm_b_im, ssm_c_re, ssm_c_im, ssm_d, w_glu_a, w_glu_b, na_rpb, w_out_c, w_o, norm_mix_post, norm_ffn2_pre, norm_ffn2_post, ffn2_w_gate, ffn2_w_up, ffn2_w_down):
    batch, seq, d = x.shape
    depth = w_in.shape[0]
    t = batch * seq
    assert d == D_MODEL and batch % V7X_SUBLANES == 0 and seq % GRID_W == 0
    rows = seq // GRID_W
    assert rows >= NA_ROWS and rows % 2 == 0 and seq % ROW_TILE == 0 and seq % SSM_STEPS == 0
    assert t % FFN_ROW_TILE == 0

    def rows_of(v):
        return v.reshape(depth, 1, -1)

    ffn1 = (rows_of(norm_ffn1_pre), rows_of(norm_ffn1_post), _bf16(ffn1_w_gate), _bf16(ffn1_w_up),
            _bf16(ffn1_w_down))
    ffn2 = (rows_of(norm_ffn2_pre), rows_of(norm_ffn2_post), _bf16(ffn2_w_gate), _bf16(ffn2_w_up),
            _bf16(ffn2_w_down))
    g_mix_pre, g_mix_post, w_in_b = rows_of(norm_mix_pre), rows_of(norm_mix_post), _bf16(w_in)
    ldt = jnp.broadcast_to(ssm_log_dt[..., None], ssm_lam_re.shape)
    ssm_params = (_state_rows(ssm_lam_re), _state_rows(ssm_lam_im), _state_rows(ldt),
                  _block_diag_in(ssm_b_re), _block_diag_in(ssm_b_im),
                  _bf16(_block_diag_out(ssm_c_re)), _bf16(_block_diag_out(ssm_c_im)))
    rpb_ext = _rpb_by_offset(na_rpb)
    merge_params = (g_mix_pre, g_mix_post, w_in_b, conv_w, _bf16(w_out_a), rows_of(ssm_d),
                    _bf16(w_glu_a), _bf16(w_glu_b), _bf16(w_out_c), _bf16(w_o))

    xt = x.reshape(t, d)
    for i in range(depth):
        xt = _ffn(xt, *ffn1, i)
        ab, p, su, q, k, v = _proj(xt, g_mix_pre, w_in_b, batch, i)
        yf, yb = _ssm(su, *ssm_params, batch, i)
        na = _na(q, k, v, rpb_ext, batch, i)
        xt = _merge(xt, ab, p, su, yf, yb, na, *merge_params, i)
        xt = _ffn(xt, *ffn2, i)
    return xt.reshape(batch, seq, d)
```

```python
import functools

import jax
import jax.numpy as jnp
from jax import lax
from jax.experimental import pallas as pl
from jax.experimental.pallas import tpu as pltpu

F32 = jnp.float32
BF16 = jnp.bfloat16

D_MODEL = 1024
D_FF = 2816
EPS = 1e-6
CONV_WIDTH = 256
SSM_WIDTH = 256
SSM_GROUP = 16
SSM_GROUPS = SSM_WIDTH // SSM_GROUP
SSM_STATE = 64
SSM_LANES = SSM_GROUPS * SSM_STATE
NA_HEADS = 8
NA_HEAD_DIM = 64
NA_WIDTH = NA_HEADS * NA_HEAD_DIM
NA_ROWS = 8
NA_COLS = 16
GRID_W = 64
NA_KEYS = NA_ROWS * GRID_W
RPB_COLS = 2 * NA_COLS - 1
PROJ_COLS = 3 * CONV_WIDTH + SSM_WIDTH + 3 * NA_WIDTH
MASK_VALUE = -1e30

V7X_VMEM_BYTES = 64 * 1024 * 1024
V7X_LANES = 128
V7X_SUBLANES = 8

NA_PAIRS = NA_WIDTH // V7X_LANES
FFN_CHUNK = 256
ROW_TILE = 512
PROJ_ROW_TILE = 1024
FFN_ROW_TILE = 1024
SSM_STEPS = 64


def _vmem_limit(nbytes):
    return int(min(V7X_VMEM_BYTES - 8 * 1024 * 1024, max(nbytes, 16 * 1024 * 1024)))


def _rms(x, g):
    ms = jnp.mean(x * x, axis=-1, keepdims=True)
    return x * lax.rsqrt(ms + EPS) * g


def _sigmoid(x):
    return 0.5 * jnp.tanh(0.5 * x) + 0.5


def _time_major_rows(tm, tiles_per_seq, width):
    return pl.BlockSpec((tm, width), lambda i: (i % tiles_per_seq, i // tiles_per_seq))


def _resident(shape, layer):
    nd = len(shape)
    return pl.BlockSpec((None,) + tuple(shape), lambda *_: (layer,) + (0,) * nd,
                        pipeline_mode=pl.Buffered(1))


def _ffn_kernel(x_ref, gpre_ref, gpost_ref, wg_ref, wu_ref, wd_ref, o_ref, acc_ref):
    half = x_ref.shape[0] // 2
    for r in range(2):
        rows = slice(r * half, (r + 1) * half)
        x = x_ref[rows, :]
        h = _rms(x, gpre_ref[...]).astype(BF16)
        for c in range(D_FF // FFN_CHUNK):
            sl = slice(c * FFN_CHUNK, (c + 1) * FFN_CHUNK)
            g = jnp.dot(h, wg_ref[:, sl], preferred_element_type=F32)
            u = jnp.dot(h, wu_ref[:, sl], preferred_element_type=F32)
            a = (g * jax.nn.sigmoid(g) * u).astype(BF16)
            d = jnp.dot(a, wd_ref[sl, :], preferred_element_type=F32)
            if c == 0:
                acc_ref[rows, :] = d
            else:
                acc_ref[rows, :] += d
        o_ref[rows, :] = x + 0.5 * _rms(acc_ref[rows, :], gpost_ref[...])


def _ffn(x, gpre, gpost, wg, wu, wd, layer):
    t = x.shape[0]
    tm = FFN_ROW_TILE
    row = pl.BlockSpec((tm, D_MODEL), lambda i: (i, 0))
    est = 3 * D_MODEL * D_FF * 2 + 5 * tm * D_MODEL * 4 + 8 * tm * FFN_CHUNK * 4 + (4 << 20)
    return pl.pallas_call(
        _ffn_kernel,
        grid=(t // tm,),
        in_specs=[row, _resident((1, D_MODEL), layer), _resident((1, D_MODEL), layer),
                  _resident((D_MODEL, D_FF), layer), _resident((D_MODEL, D_FF), layer),
                  _resident((D_FF, D_MODEL), layer)],
        out_specs=row,
        out_shape=jax.ShapeDtypeStruct((t, D_MODEL), F32),
        scratch_shapes=[pltpu.VMEM((tm, D_MODEL), F32)],
        compiler_params=pltpu.CompilerParams(
            dimension_semantics=("parallel",), vmem_limit_bytes=_vmem_limit(est)),
        name="ffn",
    )(x, gpre, gpost, wg, wu, wd)


def _proj_kernel(x_ref, g_ref, w_ref, ab_ref, p_ref, su_ref, q_ref, k_ref, v_ref):
    h = _rms(x_ref[...], g_ref[...]).astype(BF16)
    cw = CONV_WIDTH
    ab_ref[...] = jnp.dot(h, w_ref[:, 0:cw], preferred_element_type=F32)
    a_c = jnp.dot(h, w_ref[:, cw:2 * cw], preferred_element_type=F32)
    a_v = jnp.dot(h, w_ref[:, 2 * cw:3 * cw], preferred_element_type=F32)
    p_ref[...] = a_c * a_v
    o = 3 * cw
    su_ref[...] = jnp.dot(h, w_ref[:, o:o + SSM_WIDTH], preferred_element_type=F32)
    o += SSM_WIDTH
    q = jnp.dot(h, w_ref[:, o:o + NA_WIDTH], preferred_element_type=F32)
    q_ref[...] = (q * (NA_HEAD_DIM ** -0.5)).astype(BF16)
    o += NA_WIDTH
    k_ref[...] = jnp.dot(h, w_ref[:, o:o + NA_WIDTH], preferred_element_type=F32).astype(BF16)
    o += NA_WIDTH
    v_ref[...] = jnp.dot(h, w_ref[:, o:o + NA_WIDTH], preferred_element_type=F32).astype(BF16)


def _proj(x, g, w, batch, layer):
    t = x.shape[0]
    tm = PROJ_ROW_TILE
    seq = t // batch

    def rows(c):
        return pl.BlockSpec((tm, c), lambda i: (i, 0))

    est = D_MODEL * w.shape[-1] * 2 + 2 * tm * (D_MODEL + 3 * CONV_WIDTH + 3 * NA_WIDTH) * 4 + (8 << 20)
    return pl.pallas_call(
        _proj_kernel,
        grid=(t // tm,),
        in_specs=[rows(D_MODEL), _resident((1, D_MODEL), layer), _resident(w.shape[1:], layer)],
        out_specs=[rows(CONV_WIDTH), rows(CONV_WIDTH), _time_major_rows(tm, seq // tm, SSM_WIDTH),
                   rows(NA_WIDTH), rows(NA_WIDTH), rows(NA_WIDTH)],
        out_shape=[jax.ShapeDtypeStruct((t, CONV_WIDTH), F32), jax.ShapeDtypeStruct((t, CONV_WIDTH), F32),
                   jax.ShapeDtypeStruct((seq, batch * SSM_WIDTH), F32), jax.ShapeDtypeStruct((t, NA_WIDTH), BF16),
                   jax.ShapeDtypeStruct((t, NA_WIDTH), BF16), jax.ShapeDtypeStruct((t, NA_WIDTH), BF16)],
        compiler_params=pltpu.CompilerParams(
            dimension_semantics=("parallel",), vmem_limit_bytes=_vmem_limit(est)),
        name="proj",
    )(x, g, w)


def _ssm_kernel(uf_ref, ub_ref, lre_ref, lim_ref, ldt_ref, bre_ref, bim_ref, cre_ref, cim_ref,
                yf_ref, yb_ref, a_scr, bb_scr, bu_scr, st_scr, u_scr, y_scr, *, batch, steps):
    slabs = SSM_WIDTH // V7X_LANES
    n = SSM_LANES
    halves = batch // V7X_SUBLANES

    @pl.when(pl.program_id(0) == 0)
    def _discretise():
        st_scr[...] = jnp.zeros_like(st_scr)
        for d in range(2):
            lr = jnp.minimum(lre_ref[d], -1e-4)
            li = lim_ref[d]
            dt = jnp.exp(ldt_ref[d])
            mag = jnp.exp(lr * dt)
            a_re = mag * jnp.cos(li * dt)
            a_im = mag * jnp.sin(li * dt)
            den = lr * lr + li * li
            nr, ni = a_re - 1.0, a_im
            f_re = (nr * lr + ni * li) / den
            f_im = (ni * lr - nr * li) / den
            a_scr[d, 0] = a_re
            a_scr[d, 1] = a_im
            fr, fi = f_re[0:1], f_im[0:1]
            br, bi = bre_ref[d], bim_ref[d]
            bb_scr[d, :, 0:n] = (fr * br - fi * bi).astype(BF16)
            bb_scr[d, :, n:2 * n] = (fr * bi + fi * br).astype(BF16)

    def run(d, u_ref, y_ref, reverse):
        for b in range(batch):
            for s in range(slabs):
                lanes = slice(b * SSM_WIDTH + s * V7X_LANES, b * SSM_WIDTH + (s + 1) * V7X_LANES)
                u_scr[d, s, pl.ds(b, steps, stride=batch), :] = u_ref[:, lanes]
        u = jnp.concatenate([u_scr[d, s] for s in range(slabs)], axis=1).astype(BF16)
        bu = bu_scr.at[d]
        bu[...] = jnp.dot(u, bb_scr[d], preferred_element_type=F32)
        a_re = a_scr[d, 0]
        a_im = a_scr[d, 1]

        def body(k, carry):
            t = (steps - 1 - k) if reverse else k
            r0 = pl.multiple_of(t * batch, batch)
            new = []
            for hf in range(halves):
                s_re, s_im = carry[2 * hf], carry[2 * hf + 1]
                rows = pl.ds(r0 + V7X_SUBLANES * hf, V7X_SUBLANES)
                n_re = a_re * s_re - a_im * s_im + bu[rows, 0:n]
                n_im = a_re * s_im + a_im * s_re + bu[rows, n:2 * n]
                bu[rows, 0:n] = n_re
                bu[rows, n:2 * n] = n_im
                new += [n_re, n_im]
            return tuple(new)

        init = tuple(st_scr[d, j] for j in range(2 * halves))
        fin = lax.fori_loop(0, steps, body, init, unroll=True)
        for j in range(2 * halves):
            st_scr[d, j] = fin[j]
        s_re = bu[:, 0:n].astype(BF16)
        s_im = bu[:, n:2 * n].astype(BF16)
        y = (jnp.dot(s_re, cre_ref[d], preferred_element_type=F32)
             - jnp.dot(s_im, cim_ref[d], preferred_element_type=F32))
        for s in range(slabs):
            y_scr[d, s] = y[:, s * V7X_LANES:(s + 1) * V7X_LANES]
        for b in range(batch):
            for s in range(slabs):
                lanes = slice(b * SSM_WIDTH + s * V7X_LANES, b * SSM_WIDTH + (s + 1) * V7X_LANES)
                y_ref[:, lanes] = y_scr[d, s, pl.ds(b, steps, stride=batch), :]

    run(0, uf_ref, yf_ref, False)
    run(1, ub_ref, yb_ref, True)


def _ssm(su, lre, lim, ldt, bre, bim, cre, cim, batch, layer):
    seq = su.shape[0]
    steps = SSM_STEPS
    r = steps * batch
    nc = seq // steps
    n = SSM_LANES
    halves = batch // V7X_SUBLANES
    fwd = pl.BlockSpec((steps, batch * SSM_WIDTH), lambda i: (i, 0))
    bwd = pl.BlockSpec((steps, batch * SSM_WIDTH), lambda i: (nc - 1 - i, 0))
    est = (2 * SSM_WIDTH * n * 4 * 2 + 2 * n * SSM_WIDTH * 2 * 2 + 2 * SSM_WIDTH * 2 * n * 2
           + 4 * r * 2 * n * 4 + 16 * r * SSM_WIDTH * 4 + (8 << 20))
    slab = pltpu.VMEM((2, SSM_WIDTH // V7X_LANES, r, V7X_LANES), F32)
    return pl.pallas_call(
        functools.partial(_ssm_kernel, batch=batch, steps=steps),
        grid=(nc,),
        in_specs=[fwd, bwd] + [_resident((2, V7X_SUBLANES, n), layer)] * 3
                 + [_resident((2, SSM_WIDTH, n), layer)] * 2 + [_resident((2, n, SSM_WIDTH), layer)] * 2,
        out_specs=[fwd, bwd],
        out_shape=[jax.ShapeDtypeStruct((seq, batch * SSM_WIDTH), F32)] * 2,
        scratch_shapes=[pltpu.VMEM((2, 2, V7X_SUBLANES, n), F32),
                        pltpu.VMEM((2, SSM_WIDTH, 2 * n), BF16),
                        pltpu.VMEM((2, r, 2 * n), F32),
                        pltpu.VMEM((2, 2 * halves, V7X_SUBLANES, n), F32), slab, slab],
        compiler_params=pltpu.CompilerParams(
            dimension_semantics=("arbitrary",), vmem_limit_bytes=_vmem_limit(est)),
        name="ssm",
    )(su, su, lre, lim, ldt, bre, bim, cre, cim)


def _na_kernel(q_ref, k_ref, v_ref, rpb_ref, o_ref, s_scr, bias_scr, *, rows):
    lane = lax.broadcasted_iota(jnp.int32, (GRID_W, V7X_LANES), 1)
    first = lane < NA_HEAD_DIM
    zero = jnp.zeros((GRID_W, V7X_LANES), BF16)

    @pl.when(pl.program_id(0) == 0)
    def _build_bias():
        qc = lax.broadcasted_iota(jnp.int32, (GRID_W, V7X_LANES), 0)
        kc = jnp.bitwise_and(lane, GRID_W - 1)
        start = jnp.clip(qc - NA_COLS // 2, 0, GRID_W - NA_COLS)
        valid = jnp.logical_and(kc >= start, kc < start + NA_COLS)
        for h in range(NA_HEADS):
            tiles = []
            for dr in range(2 * NA_ROWS - 2):
                x0 = jnp.broadcast_to(rpb_ref[h, dr:dr + 1, :], (GRID_W, V7X_LANES))
                x1 = jnp.broadcast_to(rpb_ref[h, dr + 1:dr + 2, :], (GRID_W, V7X_LANES))
                t0 = pltpu.roll(x0, GRID_W + 1, axis=1, stride=1, stride_axis=0)
                t1 = pltpu.roll(x1, 1, axis=1, stride=1, stride_axis=0)
                tiles.append(jnp.where(valid, jnp.where(first, t0, t1), MASK_VALUE))
            j, hh = divmod(h, 2)
            for off in range(NA_ROWS):
                for m in range(NA_KEYS // V7X_LANES):
                    bias_scr[j * NA_ROWS + off, hh * GRID_W:(hh + 1) * GRID_W,
                             m * V7X_LANES:(m + 1) * V7X_LANES] = tiles[2 * m - off + NA_ROWS - 1]

    def window(r):
        rs = jnp.clip(r - NA_ROWS // 2, 0, rows - NA_ROWS)
        return r - rs, pl.multiple_of(r * GRID_W, GRID_W), pl.multiple_of(rs * GRID_W, GRID_W)

    def scores(r, buf):
        off, q0, k0 = window(r)
        for j in range(NA_PAIRS):
            cols = slice(j * V7X_LANES, (j + 1) * V7X_LANES)
            qp = q_ref[pl.ds(q0, GRID_W), cols]
            q2 = jnp.concatenate([jnp.where(first, qp, zero), jnp.where(first, zero, qp)], axis=0)
            kw = k_ref[pl.ds(k0, NA_KEYS), cols]
            s = lax.dot_general(q2, kw, (((1,), (1,)), ((), ())), preferred_element_type=F32)
            s_scr[buf, j] = s + bias_scr[j * NA_ROWS + off]

    def attend(r, buf):
        _, q0, k0 = window(r)
        for j in range(NA_PAIRS):
            cols = slice(j * V7X_LANES, (j + 1) * V7X_LANES)
            s = s_scr[buf, j]
            p = jnp.exp(s - jnp.max(s, axis=-1, keepdims=True))
            inv = 1.0 / jnp.sum(p, axis=-1, keepdims=True)
            vw = v_ref[pl.ds(k0, NA_KEYS), cols]
            pv = jnp.dot(p.astype(BF16), vw, preferred_element_type=F32) * inv
            o_ref[pl.ds(q0, GRID_W), cols] = jnp.where(first, pv[:GRID_W], pv[GRID_W:]).astype(BF16)

    scores(0, 0)

    def body(rr, carry):
        r = 2 * rr
        scores(r + 1, 1)
        attend(r, 0)
        scores(jnp.minimum(r + 2, rows - 1), 0)
        attend(r + 1, 1)
        return carry

    lax.fori_loop(0, rows // 2, body, 0)


def _na(q, k, v, rpb_ext, batch, layer):
    t = q.shape[0]
    seq = t // batch
    rows = seq // GRID_W
    blk = pl.BlockSpec((seq, NA_WIDTH), lambda b: (b, 0))
    table = NA_PAIRS * NA_ROWS * 2 * GRID_W * NA_KEYS * 4
    est = 8 * seq * NA_WIDTH * 2 + table + 2 * NA_PAIRS * 2 * GRID_W * NA_KEYS * 4 + (12 << 20)
    return pl.pallas_call(
        functools.partial(_na_kernel, rows=rows),
        grid=(batch,),
        in_specs=[blk, blk, blk, _resident(rpb_ext.shape[1:], layer)],
        out_specs=blk,
        out_shape=jax.ShapeDtypeStruct((t, NA_WIDTH), BF16),
        scratch_shapes=[pltpu.VMEM((2, NA_PAIRS, 2 * GRID_W, NA_KEYS), F32),
                        pltpu.VMEM((NA_PAIRS * NA_ROWS, 2 * GRID_W, NA_KEYS), F32)],
        compiler_params=pltpu.CompilerParams(
            dimension_semantics=("arbitrary",), vmem_limit_bytes=_vmem_limit(est)),
        name="na",
    )(q, k, v, rpb_ext)


def _rpb_by_offset(rpb):
    pad = GRID_W - NA_COLS
    ext = jnp.concatenate([jnp.repeat(rpb[..., :1], pad, axis=-1), rpb,
                           jnp.repeat(rpb[..., -1:], pad + 1, axis=-1)], axis=-1)
    return jnp.pad(ext, ((0, 0), (0, 0), (0, 1), (0, 0))).astype(F32)


def _merge_kernel(x_ref, ab_ref, p_ref, pp_ref, pn_ref, su_ref, yf_ref, yb_ref, na_ref,
                  gpre_ref, gpost_ref, win_ref, convw_ref, woa_ref, dsk_ref, wga_ref, wgb_ref,
                  woc_ref, wo_ref, o_ref, *, tiles_per_seq):
    i = pl.program_id(0)
    x = x_ref[...]
    tm = x.shape[0]
    h = _rms(x, gpre_ref[...]).astype(BF16)

    p = p_ref[...]
    row = lax.broadcasted_iota(jnp.int32, p.shape, 0)
    at_start = i % tiles_per_seq == 0
    at_end = i % tiles_per_seq == tiles_per_seq - 1
    p_prev = jnp.where(at_start, 0.0, pp_ref[V7X_SUBLANES - 1:V7X_SUBLANES, :])
    p_next = jnp.where(at_end, 0.0, pn_ref[0:1, :])
    before = jnp.where(row == 0, p_prev, pltpu.roll(p, 1, axis=0))
    after = jnp.where(row == tm - 1, p_next, pltpu.roll(p, tm - 1, axis=0))
    cw = convw_ref[...]
    conv = cw[0:1] * before + cw[1:2] * p + cw[2:3] * after
    y_a = jnp.dot((ab_ref[...] * conv).astype(BF16), woa_ref[...], preferred_element_type=F32)
    g_a = jnp.dot(h, win_ref[:, PROJ_COLS:PROJ_COLS + D_MODEL], preferred_element_type=F32)
    mix = _sigmoid(g_a) * y_a

    y_s = dsk_ref[...] * su_ref[...] + yf_ref[...] + yb_ref[...]
    z = jax.nn.gelu(y_s).astype(BF16)
    y_b = (jnp.dot(z, wga_ref[...], preferred_element_type=F32)
           * _sigmoid(jnp.dot(z, wgb_ref[...], preferred_element_type=F32)))
    g_b = jnp.dot(h, win_ref[:, PROJ_COLS + D_MODEL:PROJ_COLS + 2 * D_MODEL], preferred_element_type=F32)
    mix = mix + _sigmoid(g_b) * y_b

    y_c = jnp.dot(na_ref[...], woc_ref[...], preferred_element_type=F32)
    g_c = jnp.dot(h, win_ref[:, PROJ_COLS + 2 * D_MODEL:PROJ_COLS + 3 * D_MODEL], preferred_element_type=F32)
    mix = mix + _sigmoid(g_c) * y_c

    y = jnp.dot(mix.astype(BF16), wo_ref[...], preferred_element_type=F32)
    o_ref[...] = x + _rms(y, gpost_ref[...])


def _merge(x, ab, p, su, yf, yb, na, gpre, gpost, win, convw, woa, dsk, wga, wgb, woc, wo, layer):
    t = x.shape[0]
    seq = su.shape[0]
    tm = ROW_TILE
    tmaj = _time_major_rows(tm, seq // tm, SSM_WIDTH)
    per = tm // V7X_SUBLANES
    last = t // V7X_SUBLANES - 1

    def rows(c):
        return pl.BlockSpec((tm, c), lambda i: (i, 0))

    prev = pl.BlockSpec((V7X_SUBLANES, CONV_WIDTH), lambda i: (jnp.maximum(i * per - 1, 0), 0))
    nxt = pl.BlockSpec((V7X_SUBLANES, CONV_WIDTH), lambda i: (jnp.minimum((i + 1) * per, last), 0))
    wbytes = (D_MODEL * win.shape[-1] + CONV_WIDTH * D_MODEL + 2 * SSM_WIDTH * D_MODEL
              + NA_WIDTH * D_MODEL + D_MODEL * D_MODEL) * 2
    est = wbytes + 2 * tm * (2 * D_MODEL + 5 * CONV_WIDTH) * 4 + 10 * tm * D_MODEL * 4 + (8 << 20)
    return pl.pallas_call(
        functools.partial(_merge_kernel, tiles_per_seq=seq // tm),
        grid=(t // tm,),
        in_specs=[rows(D_MODEL), rows(CONV_WIDTH), rows(CONV_WIDTH), prev, nxt, tmaj, tmaj, tmaj,
                  rows(NA_WIDTH),
                  _resident((1, D_MODEL), layer), _resident((1, D_MODEL), layer),
                  _resident(win.shape[1:], layer), _resident((3, CONV_WIDTH), layer),
                  _resident((CONV_WIDTH, D_MODEL), layer), _resident((1, SSM_WIDTH), layer),
                  _resident((SSM_WIDTH, D_MODEL), layer), _resident((SSM_WIDTH, D_MODEL), layer),
                  _resident((NA_WIDTH, D_MODEL), layer), _resident((D_MODEL, D_MODEL), layer)],
        out_specs=rows(D_MODEL),
        out_shape=jax.ShapeDtypeStruct((t, D_MODEL), F32),
        compiler_params=pltpu.CompilerParams(
            dimension_semantics=("parallel",), vmem_limit_bytes=_vmem_limit(est)),
        name="merge",
    )(x, ab, p, p, p, su, yf, yb, na, gpre, gpost, win, convw, woa, dsk, wga, wgb, woc, wo)


def _state_rows(a):
    lead = a.shape[:-2]
    return jnp.broadcast_to(a.reshape(lead + (1, SSM_LANES)), lead + (V7X_SUBLANES, SSM_LANES))


def _block_diag_in(b):
    eye = jnp.eye(SSM_GROUPS, dtype=b.dtype)
    out = jnp.swapaxes(b, -1, -2)[..., :, :, None, :] * eye[:, None, :, None]
    return out.reshape(b.shape[:-3] + (SSM_WIDTH, SSM_LANES))


def _block_diag_out(c):
    eye = jnp.eye(SSM_GROUPS, dtype=c.dtype)
    out = jnp.swapaxes(c, -1, -2)[..., :, :, None, :] * eye[:, None, :, None]
    return out.reshape(c.shape[:-3] + (SSM_LANES, SSM_WIDTH))


def _bf16(w):
    return w.astype(BF16)


def kernel(x, norm_ffn1_pre, norm_ffn1_post, ffn1_w_gate, ffn1_w_up, ffn1_w_down, norm_mix_pre, w_in, conv_w, w_out_a, ssm_lam_re, ssm_lam_im, ssm_log_dt, ssm_b_re, ssm_b_im, ssm_c_re, ssm_c_im, ssm_d, w_glu_a, w_glu_b, na_rpb, w_out_c, w_o, norm_mix_post, norm_ffn2_pre, norm_ffn2_post, ffn2_w_gate, ffn2_w_up, ffn2_w_down):
    batch, seq, d = x.shape
    depth = w_in.shape[0]
    t = batch * seq
    assert d == D_MODEL and batch % V7X_SUBLANES == 0 and seq % GRID_W == 0
    rows = seq // GRID_W
    assert rows >= NA_ROWS and rows % 2 == 0 and seq % ROW_TILE == 0 and seq % SSM_STEPS == 0
    assert t % FFN_ROW_TILE == 0 and seq % PROJ_ROW_TILE == 0

    def rows_of(v):
        return v.reshape(depth, 1, -1)

    ffn1 = (rows_of(norm_ffn1_pre), rows_of(norm_ffn1_post), _bf16(ffn1_w_gate), _bf16(ffn1_w_up),
            _bf16(ffn1_w_down))
    ffn2 = (rows_of(norm_ffn2_pre), rows_of(norm_ffn2_post), _bf16(ffn2_w_gate), _bf16(ffn2_w_up),
            _bf16(ffn2_w_down))
    g_mix_pre, g_mix_post, w_in_b = rows_of(norm_mix_pre), rows_of(norm_mix_post), _bf16(w_in)
    ldt = jnp.broadcast_to(ssm_log_dt[..., None], ssm_lam_re.shape)
    ssm_params = (_state_rows(ssm_lam_re), _state_rows(ssm_lam_im), _state_rows(ldt),
                  _block_diag_in(ssm_b_re), _block_diag_in(ssm_b_im),
                  _bf16(_block_diag_out(ssm_c_re)), _bf16(_block_diag_out(ssm_c_im)))
    rpb_ext = _rpb_by_offset(na_rpb)
    merge_params = (g_mix_pre, g_mix_post, w_in_b, conv_w, _bf16(w_out_a), rows_of(ssm_d),
                    _bf16(w_glu_a), _bf16(w_glu_b), _bf16(w_out_c), _bf16(w_o))

    xt = x.reshape(t, d)
    for i in range(depth):
        xt = _ffn(xt, *ffn1, i)
        ab, p, su, q, k, v = _proj(xt, g_mix_pre, w_in_b, batch, i)
        yf, yb = _ssm(su, *ssm_params, batch, i)
        na = _na(q, k, v, rpb_ext, batch, i)
        xt = _merge(xt, ab, p, su, yf, yb, na, *merge_params, i)
        xt = _ffn(xt, *ffn2, i)
    return xt.reshape(batch, seq, d)
```
